```python
import math
import jax
import jax.numpy as jnp
from jax import lax
import numpy as np


D_MODEL = 1024
BATCH = 4
SEQ = 4096
DEPTH = 4

GRID_W = 64
CTX_LEN = 256
QBLOCK = 128

DIFF_HEADS = 4
DIFF_QK_DIM = 64
DIFF_V_DIM = 2 * DIFF_QK_DIM
DIFF_WIDTH = DIFF_HEADS * DIFF_V_DIM

S5_WIDTH = 512
S5_GROUP = 16
S5_GROUPS = S5_WIDTH // S5_GROUP
S5_STATE = 64

WIN_Q_HEADS = 8
WIN_KV_HEADS = 2
WIN_HEAD_DIM = 64
WIN_GROUP = WIN_Q_HEADS // WIN_KV_HEADS
WIN_WIDTH = WIN_Q_HEADS * WIN_HEAD_DIM
WINDOW = 128

ROPE_DIM = 64
N_BRANCHES = 3
D_FF = 4 * D_MODEL
ROPE_BASE = 10000.0
EPS = 1e-6
NEG_INF = -1e30

IN_WIDTHS = (2 * DIFF_HEADS * DIFF_QK_DIM, 2 * DIFF_HEADS * DIFF_QK_DIM, DIFF_WIDTH, S5_WIDTH,
             WIN_WIDTH, WIN_KV_HEADS * WIN_HEAD_DIM, WIN_KV_HEADS * WIN_HEAD_DIM, N_BRANCHES * D_MODEL)
D_IN = sum(IN_WIDTHS)

kernel_name = 'hybrid_diffattn_s5_swa_prefix_dit'


def _in_offsets():
    offs, acc = [], 0
    for w in IN_WIDTHS[:-1]:
        acc += w
        offs.append(acc)
    return offs


def rms_norm(x, g):
    xf = x.astype(jnp.float32)
    y = xf * lax.rsqrt(jnp.mean(jnp.square(xf), -1, keepdims=True) + EPS)
    return (y * g.astype(jnp.float32)).astype(x.dtype)


def axial_rope(rows, head_dim):
    n_freq = head_dim // 4
    inv = ROPE_BASE ** (-jnp.arange(n_freq, dtype=jnp.float32) / n_freq)
    r = jnp.repeat(jnp.arange(rows, dtype=jnp.float32), GRID_W)
    col = jnp.tile(jnp.arange(GRID_W, dtype=jnp.float32), rows)
    ang = jnp.concatenate([r[:, None] * inv, col[:, None] * inv], -1)
    return jnp.cos(ang), jnp.sin(ang)


def apply_rope(x, cos, sin):
    bshape = (1, x.shape[1]) + (1,) * (x.ndim - 3) + (cos.shape[-1],)
    cos = cos.reshape(bshape)
    sin = sin.reshape(bshape)
    x1, x2 = jnp.split(x.astype(jnp.float32), 2, -1)
    return jnp.concatenate([x1 * cos - x2 * sin, x2 * cos + x1 * sin], -1).astype(x.dtype)


def diff_attention(q_lat, k_lat, v_lat, q_ctx, k_ctx, v_ctx, cos, sin, q_g, k_g, lam, lam_init, out_g, need_ctx):
    b, n, _ = q_lat.shape
    nc = q_ctx.shape[1]
    sh = (2, DIFF_HEADS, DIFF_QK_DIM)
    scale = DIFF_QK_DIM ** -0.5
    ql = apply_rope(rms_norm(q_lat.reshape((b, n) + sh), q_g), cos, sin)
    kl = apply_rope(rms_norm(k_lat.reshape((b, n) + sh), k_g), cos, sin)
    kc = rms_norm(k_ctx.reshape((b, nc) + sh), k_g)
    vl = v_lat.reshape(b, n, DIFF_HEADS, DIFF_V_DIM)
    vc = v_ctx.reshape(b, nc, DIFF_HEADS, DIFF_V_DIM)
    k_all = jnp.concatenate([kc, kl], 1)
    v_all = jnp.concatenate([vc, vl], 1)

    def attend(q, k, v):
        s = jnp.einsum('bqmhd,bkmhd->bmhqk', q, k).astype(jnp.float32) * scale
        p = jax.nn.softmax(s, -1)
        p = p[:, 0] - lam * p[:, 1]
        return jnp.einsum('bhqk,bkhe->bqhe', p.astype(v.dtype), v)

    def post(o, length):
        o = rms_norm(o, out_g) * (1.0 - lam_init)
        return o.reshape(b, length, DIFF_WIDTH)

    nb = n // QBLOCK
    q_blocks = ql.reshape((b, nb, QBLOCK) + sh).swapaxes(0, 1)
    o = lax.map(lambda qb: attend(qb, k_all, v_all), q_blocks)
    y_lat = post(o.swapaxes(0, 1).reshape(b, n, DIFF_HEADS, DIFF_V_DIM), n)
    y_ctx = None
    if need_ctx:
        qc = rms_norm(q_ctx.reshape((b, nc) + sh), q_g)
        y_ctx = post(attend(qc, kc, vc), nc)
    return y_lat, y_ctx


def window_gqa(q_lat, k_lat, v_lat, q_ctx, k_ctx, v_ctx, cos, sin, q_g, k_g, sink, need_ctx):
    b, n, _ = q_lat.shape
    nc = q_ctx.shape[1]
    scale = WIN_HEAD_DIM ** -0.5
    ql = apply_rope(rms_norm(q_lat.reshape(b, n, WIN_KV_HEADS, WIN_GROUP, WIN_HEAD_DIM), q_g), cos, sin)
    kl = apply_rope(rms_norm(k_lat.reshape(b, n, WIN_KV_HEADS, WIN_HEAD_DIM), k_g), cos, sin)
    vl = v_lat.reshape(b, n, WIN_KV_HEADS, WIN_HEAD_DIM)
    kc = rms_norm(k_ctx.reshape(b, nc, WIN_KV_HEADS, WIN_HEAD_DIM), k_g)
    vc = v_ctx.reshape(b, nc, WIN_KV_HEADS, WIN_HEAD_DIM)
    sink_l = sink.astype(jnp.float32).reshape(WIN_KV_HEADS, WIN_GROUP)

    nb = n // QBLOCK
    qb = ql.reshape(b, nb, QBLOCK, WIN_KV_HEADS, WIN_GROUP, WIN_HEAD_DIM)
    pad = ((0, 0), (QBLOCK, QBLOCK), (0, 0), (0, 0))
    kp = jnp.pad(kl, pad).reshape(b, nb + 2, QBLOCK, WIN_KV_HEADS, WIN_HEAD_DIM)
    vp = jnp.pad(vl, pad).reshape(b, nb + 2, QBLOCK, WIN_KV_HEADS, WIN_HEAD_DIM)
    k_band = jnp.concatenate([kp[:, :-2], kp[:, 1:-1], kp[:, 2:]], 2)
    v_band = jnp.concatenate([vp[:, :-2], vp[:, 1:-1], vp[:, 2:]], 2)
    qi = jnp.arange(QBLOCK)
    sj = jnp.arange(3 * QBLOCK)
    key_pos = jnp.arange(nb)[:, None] * QBLOCK - QBLOCK + sj[None, :]
    rel = sj[None, :] - QBLOCK - qi[:, None]
    valid = (jnp.abs(rel) <= WINDOW)[None] & ((key_pos >= 0) & (key_pos < n))[:, None, :]

    s_band = jnp.einsum('bnqkgd,bnskd->bnkgqs', qb, k_band).astype(jnp.float32) * scale
    s_band = jnp.where(valid[None, :, None, None], s_band, NEG_INF)
    s_ctx = jnp.einsum('bnqkgd,bskd->bnkgqs', qb, kc).astype(jnp.float32) * scale
    s_sink = jnp.broadcast_to(sink_l[None, None, :, :, None, None], s_ctx.shape[:-1] + (1,))
    p = jax.nn.softmax(jnp.concatenate([s_ctx, s_band, s_sink], -1), -1).astype(vl.dtype)
    o = (jnp.einsum('bnkgqs,bskd->bnqkgd', p[..., :nc], vc)
         + jnp.einsum('bnkgqs,bnskd->bnqkgd', p[..., nc:nc + 3 * QBLOCK], v_band))
    y_lat = o.reshape(b, n, WIN_WIDTH)
    y_ctx = None
    if need_ctx:
        qc = rms_norm(q_ctx.reshape(b, nc, WIN_KV_HEADS, WIN_GROUP, WIN_HEAD_DIM), q_g)
        s = jnp.einsum('bqkgd,bskd->bkgqs', qc, kc).astype(jnp.float32) * scale
        s_sk = jnp.broadcast_to(sink_l[None, :, :, None, None], s.shape[:-1] + (1,))
        pc = jax.nn.softmax(jnp.concatenate([s, s_sk], -1), -1).astype(vc.dtype)
        y_ctx = jnp.einsum('bkgqs,bskd->bqkgd', pc[..., :nc], vc).reshape(b, nc, WIN_WIDTH)
    return y_lat, y_ctx


def s5_discretize(lam_re, lam_im, log_dt, b_re, b_im):
    dt = jnp.exp(log_dt)[:, None]
    mag = jnp.exp(lam_re * dt)
    ang = lam_im * dt
    a_re = mag * jnp.cos(ang)
    a_im = mag * jnp.sin(ang)
    den = jnp.square(lam_re) + jnp.square(lam_im)
    n_re = a_re - 1.0
    f_re = (n_re * lam_re + a_im * lam_im) / den
    f_im = (a_im * lam_re - n_re * lam_im) / den
    bb_re = f_re[..., None] * b_re - f_im[..., None] * b_im
    bb_im = f_re[..., None] * b_im + f_im[..., None] * b_re
    return a_re, a_im, bb_re, bb_im


def _ssm_combine(left, right):
    a1r, a1i, b1r, b1i = left
    a2r, a2i, b2r, b2i = right
    return (a2r * a1r - a2i * a1i, a2r * a1i + a2i * a1r,
            a2r * b1r - a2i * b1i + b2r, a2r * b1i + a2i * b1r + b2i)


def s5_scan(a_re, a_im, bu_re, bu_im, reverse, init=None):
    if init is not None:
        idx = bu_re.shape[1] - 1 if reverse else 0
        s_re, s_im = init
        bu_re = bu_re.at[:, idx].add(a_re * s_re - a_im * s_im)
        bu_im = bu_im.at[:, idx].add(a_re * s_im + a_im * s_re)
    ar = jnp.broadcast_to(a_re, bu_re.shape)
    ai = jnp.broadcast_to(a_im, bu_re.shape)
    _, _, s_re, s_im = lax.associative_scan(_ssm_combine, (ar, ai, bu_re, bu_im), reverse=reverse, axis=1)
    return s_re, s_im


def s5_readout(c_re, c_im, s_re, s_im):
    y = jnp.einsum('ghp,blgp->blgh', c_re, s_re) - jnp.einsum('ghp,blgp->blgh', c_im, s_im)
    return y.reshape(y.shape[0], y.shape[1], S5_WIDTH)


def s5_mixer(u_lat, u_ctx, lam_re, lam_im, log_dt, b_re, b_im, c_re, c_im, d_skip, w_glu, need_ctx):
    f32 = jnp.float32
    b, n, _ = u_lat.shape
    nc = u_ctx.shape[1]
    ul = u_lat.astype(f32).reshape(b, n, S5_GROUPS, S5_GROUP)
    uc = u_ctx.astype(f32).reshape(b, nc, S5_GROUPS, S5_GROUP)
    dsk = d_skip.astype(f32)
    y_lat = u_lat.astype(f32) * dsk
    y_ctx = u_ctx.astype(f32) * dsk if need_ctx else None
    for direction, reverse in ((0, False), (1, True)):
        a_re, a_im, bb_re, bb_im = s5_discretize(lam_re[direction].astype(f32), lam_im[direction].astype(f32),
                                                 log_dt[direction].astype(f32), b_re[direction].astype(f32),
                                                 b_im[direction].astype(f32))
        cr = c_re[direction].astype(f32)
        ci = c_im[direction].astype(f32)
        cs_re, cs_im = s5_scan(a_re, a_im, jnp.einsum('gph,blgh->blgp', bb_re, uc),
                               jnp.einsum('gph,blgh->blgp', bb_im, uc), reverse)
        edge = 0 if reverse else nc - 1
        ls_re, ls_im = s5_scan(a_re, a_im, jnp.einsum('gph,blgh->blgp', bb_re, ul),
                               jnp.einsum('gph,blgh->blgp', bb_im, ul), reverse,
                               init=(cs_re[:, edge], cs_im[:, edge]))
        y_lat = y_lat + s5_readout(cr, ci, ls_re, ls_im)
        if need_ctx:
            y_ctx = y_ctx + s5_readout(cr, ci, cs_re, cs_im)

    def glu(y):
        g = jax.nn.gelu(y.astype(u_lat.dtype))
        return g * jax.nn.sigmoid(g @ w_glu)

    return glu(y_lat), (glu(y_ctx) if need_ctx else None)


def setup_inputs(seed: int = 0) -> dict:
    key = jax.random.key(seed)
    ks = iter(jax.random.split(key, 48))
    f32 = jnp.float32

    def nrm(shape, scale):
        return jax.random.normal(next(ks), shape, f32) * scale

    def gain(shape):
        return 1.0 + nrm(shape, 0.02)

    L, D = DEPTH, D_MODEL
    G, P, HG = S5_GROUPS, S5_STATE, S5_GROUP
    n_idx = jnp.arange(P, dtype=f32)
    return {
        'x': nrm((BATCH, SEQ, D), 1.0),
        'c': nrm((BATCH, D), 1.0),
        'ctx': nrm((BATCH, CTX_LEN, D), 1.0),
        'c_ctx': nrm((D,), 1.0),
        'w_mod': nrm((L, D, 6 * D), 0.5 * D ** -0.5),
        'b_mod': nrm((L, 6 * D), 0.02),
        'norm1_g': gain((L, D)),
        'norm2_g': gain((L, D)),
        'w_in': nrm((L, D, D_IN), D ** -0.5),
        'diff_q_norm_g': gain((L, DIFF_QK_DIM)),
        'diff_k_norm_g': gain((L, DIFF_QK_DIM)),
        'diff_lam_q1': nrm((L, DIFF_QK_DIM), 0.1),
        'diff_lam_k1': nrm((L, DIFF_QK_DIM), 0.1),
        'diff_lam_q2': nrm((L, DIFF_QK_DIM), 0.1),
        'diff_lam_k2': nrm((L, DIFF_QK_DIM), 0.1),
        'diff_out_norm_g': gain((L, DIFF_V_DIM)),
        's5_lambda_re': -0.5 + nrm((L, 2, G, P), 0.01),
        's5_lambda_im': math.pi * n_idx + nrm((L, 2, G, P), 0.01),
        's5_log_dt': jax.random.uniform(next(ks), (L, 2, G), f32, math.log(1e-3), math.log(1e-1)),
        's5_b_re': nrm((L, 2, G, P, HG), (2 * HG) ** -0.5),
        's5_b_im': nrm((L, 2, G, P, HG), (2 * HG) ** -0.5),
        's5_c_re': nrm((L, 2, G, HG, P), (2 * P) ** -0.5 * 4.0),
        's5_c_im': nrm((L, 2, G, HG, P), (2 * P) ** -0.5 * 4.0),
        's5_d': nrm((L, S5_WIDTH), 1.0),
        's5_w_glu': nrm((L, S5_WIDTH, S5_WIDTH), S5_WIDTH ** -0.5),
        'win_q_norm_g': gain((L, WIN_HEAD_DIM)),
        'win_k_norm_g': gain((L, WIN_HEAD_DIM)),
        'win_sink': nrm((L, WIN_Q_HEADS), 1.0),
        'w_proj_diff': nrm((L, DIFF_WIDTH, D), DIFF_WIDTH ** -0.5),
        'w_proj_s5': nrm((L, S5_WIDTH, D), S5_WIDTH ** -0.5),
        'w_proj_win': nrm((L, WIN_WIDTH, D), WIN_WIDTH ** -0.5),
        'w_out': nrm((L, D, D), D ** -0.5),
        'w_ff1': nrm((L, D, D_FF), D ** -0.5),
        'w_ff2': nrm((L, D_FF, D), D_FF ** -0.5),
    }


def reference(x, c, ctx, c_ctx, w_mod, b_mod, norm1_g, norm2_g, w_in,
              diff_q_norm_g, diff_k_norm_g, diff_lam_q1, diff_lam_k1, diff_lam_q2, diff_lam_k2, diff_out_norm_g,
              s5_lambda_re, s5_lambda_im, s5_log_dt, s5_b_re, s5_b_im, s5_c_re, s5_c_im, s5_d, s5_w_glu,
              win_q_norm_g, win_k_norm_g, win_sink,
              w_proj_diff, w_proj_s5, w_proj_win, w_out, w_ff1, w_ff2):
    n_lat = x.shape[1]
    ROWS = n_lat // GRID_W
    cos, sin = axial_rope(ROWS, ROPE_DIM)
    offs = _in_offsets()
    silu_c = jax.nn.silu(c)
    silu_cc = jax.nn.silu(c_ctx)
    h_lat, h_ctx = x, ctx
    for l in range(DEPTH):
        need_ctx = l < DEPTH - 1
        lam_init = 0.8 - 0.6 * math.exp(-0.3 * l)
        mod_l = (silu_c @ w_mod[l] + b_mod[l])[:, None, :]
        mod_c = (silu_cc @ w_mod[l] + b_mod[l])[None, None, :]
        sh1, sc1, g1, sh2, sc2, g2 = jnp.split(mod_l, 6, -1)
        csh1, csc1, cg1, csh2, csc2, cg2 = jnp.split(mod_c, 6, -1)

        a_lat = rms_norm(h_lat, norm1_g[l]) * (1.0 + sc1) + sh1
        a_ctx = rms_norm(h_ctx, norm1_g[l]) * (1.0 + csc1) + csh1
        dq_l, dk_l, dv_l, su_l, wq_l, wk_l, wv_l, gt_l = jnp.split(a_lat @ w_in[l], offs, -1)
        dq_c, dk_c, dv_c, su_c, wq_c, wk_c, wv_c, gt_c = jnp.split(a_ctx @ w_in[l], offs, -1)

        lam = (jnp.exp(jnp.sum(diff_lam_q1[l] * diff_lam_k1[l]).astype(jnp.float32))
               - jnp.exp(jnp.sum(diff_lam_q2[l] * diff_lam_k2[l]).astype(jnp.float32)) + lam_init)
        yd_l, yd_c = diff_attention(dq_l, dk_l, dv_l, dq_c, dk_c, dv_c, cos, sin, diff_q_norm_g[l],
                                    diff_k_norm_g[l], lam, lam_init, diff_out_norm_g[l], need_ctx)
        ys_l, ys_c = s5_mixer(su_l, su_c, s5_lambda_re[l], s5_lambda_im[l], s5_log_dt[l], s5_b_re[l],
                              s5_b_im[l], s5_c_re[l], s5_c_im[l], s5_d[l], s5_w_glu[l], need_ctx)
        yw_l, yw_c = window_gqa(wq_l, wk_l, wv_l, wq_c, wk_c, wv_c, cos, sin, win_q_norm_g[l],
                                win_k_norm_g[l], win_sink[l], need_ctx)

        def merge(yd, ys, yw, gates):
            ga, gb, gc = jnp.split(jax.nn.sigmoid(gates), N_BRANCHES, -1)
            m = ga * (yd @ w_proj_diff[l]) + gb * (ys @ w_proj_s5[l]) + gc * (yw @ w_proj_win[l])
            return m @ w_out[l]

        h_lat = h_lat + g1 * merge(yd_l, ys_l, yw_l, gt_l)
        f_lat = rms_norm(h_lat, norm2_g[l]) * (1.0 + sc2) + sh2
        h_lat = h_lat + g2 * (jnp.square(jax.nn.relu(f_lat @ w_ff1[l])) @ w_ff2[l])
        if need_ctx:
            h_ctx = h_ctx + cg1 * merge(yd_c, ys_c, yw_c, gt_c)
            f_ctx = rms_norm(h_ctx, norm2_g[l]) * (1.0 + csc2) + csh2
            h_ctx = h_ctx + cg2 * (jnp.square(jax.nn.relu(f_ctx @ w_ff1[l])) @ w_ff2[l])
    return h_lat
```

```python
import functools
import math

import numpy as np
import jax
import jax.numpy as jnp
from jax import lax
from jax.experimental import pallas as pl
from jax.experimental.pallas import tpu as pltpu

F32 = jnp.float32
BF16 = jnp.bfloat16

GRID_W = 64
QBLOCK = 128
DIFF_HEADS = 4
DIFF_QK_DIM = 64
DIFF_V_DIM = 128
DIFF_WIDTH = DIFF_HEADS * DIFF_V_DIM
S5_WIDTH = 512
S5_GROUP = 16
S5_GROUPS = 32
S5_STATE = 64
WIN_Q_HEADS = 8
WIN_KV_HEADS = 2
WIN_HEAD_DIM = 64
WIN_GROUP = WIN_Q_HEADS // WIN_KV_HEADS
WIN_WIDTH = WIN_Q_HEADS * WIN_HEAD_DIM
WINDOW = 128
ROPE_DIM = 64
ROPE_BASE = 10000.0
EPS = 1e-6
NEG_INF = -1e30

LANES = 128
SUBLANES = 8
ROW_TILE = 256
S5_CHUNK = 128
S5_BLOCK_GROUPS = LANES // S5_GROUP
S5_BLOCKS = S5_GROUPS // S5_BLOCK_GROUPS
S5_BLOCK_STATES = S5_BLOCK_GROUPS * S5_STATE
VMEM_LIMIT = 48 * 1024 * 1024

N_NORM_CHUNKS = 13
OFF_QD, OFF_KD, OFF_QW, OFF_KW = 0, 512, 1024, 1536
OFF_VD, OFF_VW, OFF_SU, OFF_GT = 1664, 2176, 2304, 2816


def _sigmoid(x):
    return 1.0 / (1.0 + jnp.exp(-x))


def _rms(x, g):
    return x * lax.rsqrt(jnp.mean(x * x, axis=-1, keepdims=True) + EPS) * g


def _const_spec(shape, index_map):
    return pl.BlockSpec(shape, index_map, pipeline_mode=pl.Buffered(1))


def _params(*sem):
    return pltpu.CompilerParams(dimension_semantics=sem, vmem_limit_bytes=VMEM_LIMIT)


def _mod_kernel(cc_ref, w_ref, b_ref, o_ref):
    cc = cc_ref[...]
    s = cc * _sigmoid(cc)
    o_ref[0] = jnp.dot(s.astype(BF16), w_ref[0].astype(BF16), preferred_element_type=F32) + b_ref[0]


def _modulation(cc, w_mod, b_mod):
    depth, d, d6 = w_mod.shape
    tn = 1536
    return pl.pallas_call(
        _mod_kernel,
        grid=(depth, d6 // tn),
        in_specs=[
            pl.BlockSpec((SUBLANES, d), lambda l, n: (0, 0)),
            pl.BlockSpec((1, d, tn), lambda l, n: (l, 0, n)),
            pl.BlockSpec((1, 1, tn), lambda l, n: (l, 0, n)),
        ],
        out_specs=pl.BlockSpec((1, SUBLANES, tn), lambda l, n: (l, 0, n)),
        out_shape=jax.ShapeDtypeStruct((depth, SUBLANES, d6), F32),
        compiler_params=_params("parallel", "parallel"),
        name="modulation",
    )(cc, w_mod, b_mod.reshape(depth, 1, d6))


def _inproj_kernel(h_ref, mod_ref, n1g_ref, w_ref, gtab_ref, cos_ref, sin_ref,
                   qd_ref, kdt_ref, qw_ref, kwt_ref, vd_ref, vw_ref, su_ref, gt_ref):
    x = h_ref[0]
    m = mod_ref[0, 0]
    a = (_rms(x, n1g_ref[0]) * (1.0 + m[1:2]) + m[0:1]).astype(BF16)
    tm = x.shape[0]
    lane = lax.broadcasted_iota(jnp.int32, (tm, LANES), 1)
    low_seg = lane < ROPE_DIM
    first_half = (lane % ROPE_DIM) < (ROPE_DIM // 2)
    cosf = cos_ref[...]
    sinf = sin_ref[...]
    for c in range(N_NORM_CHUNKS):
        xc = jnp.dot(a, w_ref[0, :, c * LANES:(c + 1) * LANES], preferred_element_type=F32)
        y = xc * xc
        ss_lo = jnp.sum(jnp.where(low_seg, y, 0.0), axis=-1, keepdims=True)
        ss_hi = jnp.sum(jnp.where(low_seg, 0.0, y), axis=-1, keepdims=True)
        r = jnp.where(low_seg, lax.rsqrt(ss_lo / ROPE_DIM + EPS), lax.rsqrt(ss_hi / ROPE_DIM + EPS))
        xn = xc * r * gtab_ref[0, :, c * LANES:(c + 1) * LANES]
        partner = jnp.where(first_half, pltpu.roll(xn, LANES - ROPE_DIM // 2, 1),
                            pltpu.roll(xn, ROPE_DIM // 2, 1))
        out = xn * cosf + partner * sinf
        if c < 4:
            qd_ref[0, :, c * LANES:(c + 1) * LANES] = out.astype(BF16)
        elif c < 8:
            kdt_ref[0, (c - 4) * LANES:(c - 3) * LANES, :] = out.T.astype(BF16)
        elif c < 12:
            qw_ref[0, :, (c - 8) * LANES:(c - 7) * LANES] = out.astype(BF16)
        else:
            kwt_ref[0] = out.T.astype(BF16)
    vd_ref[0] = jnp.dot(a, w_ref[0, :, OFF_VD:OFF_VW], preferred_element_type=F32).astype(BF16)
    vw_ref[0] = jnp.dot(a, w_ref[0, :, OFF_VW:OFF_SU], preferred_element_type=F32).astype(BF16)
    su_ref[0] = jnp.dot(a, w_ref[0, :, OFF_SU:OFF_GT], preferred_element_type=F32)
    gt_ref[0] = _sigmoid(jnp.dot(a, w_ref[0, :, OFF_GT:], preferred_element_type=F32))


def _inproj(l, h, mods, n1g, w_in_p, gtab, cosf, sinf):
    b, s, d = h.shape
    d_in = w_in_p.shape[-1]
    nt = s // ROW_TILE
    mod_idx = lambda bi, j: (l, jnp.where(j == 0, b, bi), 0, 0)
    tile = lambda width: pl.BlockSpec((1, ROW_TILE, width), lambda bi, j: (bi, j, 0))
    out_shapes = (
        jax.ShapeDtypeStruct((b, s, 512), BF16),
        jax.ShapeDtypeStruct((b, 512, s), BF16),
        jax.ShapeDtypeStruct((b, s, WIN_WIDTH), BF16),
        jax.ShapeDtypeStruct((b, LANES, s), BF16),
        jax.ShapeDtypeStruct((b, s, DIFF_WIDTH), BF16),
        jax.ShapeDtypeStruct((b, s, LANES), BF16),
        jax.ShapeDtypeStruct((b, s, S5_WIDTH), F32),
        jax.ShapeDtypeStruct((b, s, 3 * d), F32),
    )
    out_specs = (
        tile(512),
        pl.BlockSpec((1, 512, ROW_TILE), lambda bi, j: (bi, 0, j)),
        tile(WIN_WIDTH),
        pl.BlockSpec((1, LANES, ROW_TILE), lambda bi, j: (bi, 0, j)),
        tile(DIFF_WIDTH), tile(LANES), tile(S5_WIDTH), tile(3 * d),
    )
    return pl.pallas_call(
        _inproj_kernel,
        grid=(b, nt),
        in_specs=[
            tile(d),
            pl.BlockSpec((1, 1, 6, d), mod_idx),
            _const_spec((1, 1, d), lambda bi, j: (l, 0, 0)),
            _const_spec((1, d, d_in), lambda bi, j: (l, 0, 0)),
            _const_spec((1, 1, N_NORM_CHUNKS * LANES), lambda bi, j: (l, 0, 0)),
            pl.BlockSpec((ROW_TILE, LANES), lambda bi, j: (j, 0)),
            pl.BlockSpec((ROW_TILE, LANES), lambda bi, j: (j, 0)),
        ],
        out_specs=out_specs,
        out_shape=out_shapes,
        compiler_params=_params("parallel", "parallel"),
        name="inproj",
    )(h, mods, n1g, w_in_p, gtab, cosf, sinf)


def _diff_kernel(lam_ref, og_ref, q_ref, kt_ref, v_ref, o_ref, *, lam_init, n_ctx):
    lp = lam_ref[0]
    lam = (jnp.exp(jnp.sum(lp[0:1] * lp[1:2], axis=-1, keepdims=True))
           - jnp.exp(jnp.sum(lp[2:3] * lp[3:4], axis=-1, keepdims=True)) + lam_init)
    q = q_ref[0]
    og = og_ref[0]

    def attend(n_keys):
        outs = []
        for m in range(2):
            qm = q[:, m * DIFF_QK_DIM:(m + 1) * DIFF_QK_DIM]
            km = kt_ref[0, m * DIFF_QK_DIM:(m + 1) * DIFF_QK_DIM, 0:n_keys]
            s = jnp.dot(qm, km, preferred_element_type=F32)
            e = jnp.exp(s - jnp.max(s, axis=-1, keepdims=True))
            den = jnp.sum(e, axis=-1, keepdims=True)
            outs.append(jnp.dot(e.astype(BF16), v_ref[0, 0:n_keys, :], preferred_element_type=F32) / den)
        o = outs[0] - lam * outs[1]
        o_ref[0] = (_rms(o, og) * (1.0 - lam_init)).astype(o_ref.dtype)

    is_ctx = pl.program_id(2) == 0

    @pl.when(is_ctx)
    def _():
        attend(n_ctx)

    @pl.when(jnp.logical_not(is_ctx))
    def _():
        attend(kt_ref.shape[2])


def _diff_attention(l, lam_init, n_ctx, lam_p, out_g, qd, kdt, vd):
    b, s, _ = qd.shape
    nt = s // ROW_TILE
    kern = functools.partial(_diff_kernel, lam_init=lam_init, n_ctx=n_ctx)
    return pl.pallas_call(
        kern,
        grid=(b, DIFF_HEADS, nt),
        in_specs=[
            _const_spec((1, 4, DIFF_QK_DIM), lambda bi, hd, j: (l, 0, 0)),
            _const_spec((1, 1, DIFF_V_DIM), lambda bi, hd, j: (l, 0, 0)),
            pl.BlockSpec((1, ROW_TILE, LANES), lambda bi, hd, j: (bi, j, hd)),
            pl.BlockSpec((1, LANES, s), lambda bi, hd, j: (bi, hd, 0)),
            pl.BlockSpec((1, s, DIFF_V_DIM), lambda bi, hd, j: (bi, 0, hd)),
        ],
        out_specs=pl.BlockSpec((1, ROW_TILE, DIFF_V_DIM), lambda bi, hd, j: (bi, j, hd)),
        out_shape=jax.ShapeDtypeStruct((b, s, DIFF_WIDTH), BF16),
        compiler_params=_params("parallel", "parallel", "parallel"),
        name="diff_attention",
    )(lam_p, out_g, qd, kdt, vd)


def _win_kernel(sink_ref, q_ref, kt_ref, v_ref, o_ref, *, layer, n_ctx):
    j = pl.program_id(1)
    s_total = kt_ref.shape[2]
    band = 3 * QBLOCK
    q = q_ref[0]
    rows = WIN_GROUP * QBLOCK
    row_group = lax.broadcasted_iota(jnp.int32, (rows, 1), 0) // QBLOCK

    def run(with_band):
        for kv in range(WIN_KV_HEADS):
            q4 = jnp.concatenate(
                [q[:, (kv * WIN_GROUP + g) * WIN_HEAD_DIM:(kv * WIN_GROUP + g + 1) * WIN_HEAD_DIM]
                 for g in range(WIN_GROUP)], axis=0)
            sink = jnp.zeros((rows, 1), F32)
            for g in range(WIN_GROUP):
                sink = jnp.where(row_group == g, sink_ref[layer, kv * WIN_GROUP + g], sink)
            ksl = slice(kv * WIN_HEAD_DIM, (kv + 1) * WIN_HEAD_DIM)
            s_c = jnp.dot(q4, kt_ref[0, ksl, 0:n_ctx], preferred_element_type=F32)
            mx = jnp.maximum(jnp.max(s_c, axis=-1, keepdims=True), sink)
            if with_band:
                start = pl.multiple_of(QBLOCK * jnp.minimum(j - 1, (s_total - band) // QBLOCK), QBLOCK)
                s_b = jnp.dot(q4, kt_ref[0, ksl, pl.ds(start, band)], preferred_element_type=F32)
                qpos = QBLOCK * j + lax.broadcasted_iota(jnp.int32, (rows, band), 0) % QBLOCK
                kpos = start + lax.broadcasted_iota(jnp.int32, (rows, band), 1)
                valid = (jnp.abs(kpos - qpos) <= WINDOW) & (kpos >= n_ctx)
                s_b = jnp.where(valid, s_b, NEG_INF)
                mx = jnp.maximum(mx, jnp.max(s_b, axis=-1, keepdims=True))
            e_c = jnp.exp(s_c - mx)
            den = jnp.sum(e_c, axis=-1, keepdims=True) + jnp.exp(sink - mx)
            o = jnp.dot(e_c.astype(BF16), v_ref[0, 0:n_ctx, ksl], preferred_element_type=F32)
            if with_band:
                e_b = jnp.exp(s_b - mx)
                den = den + jnp.sum(e_b, axis=-1, keepdims=True)
                o = o + jnp.dot(e_b.astype(BF16), v_ref[0, pl.ds(start, band), ksl],
                                preferred_element_type=F32)
            o = o / den
            for g in range(WIN_GROUP):
                c0 = (kv * WIN_GROUP + g) * WIN_HEAD_DIM
                o_ref[0, :, c0:c0 + WIN_HEAD_DIM] = o[g * QBLOCK:(g + 1) * QBLOCK].astype(o_ref.dtype)

    is_ctx = j < n_ctx // QBLOCK

    @pl.when(is_ctx)
    def _():
        run(False)

    @pl.when(jnp.logical_not(is_ctx))
    def _():
        run(True)


def _window_attention(l, n_ctx, sink, qw, kwt, vw):
    b, s, _ = qw.shape
    kern = functools.partial(_win_kernel, layer=l, n_ctx=n_ctx)
    return pl.pallas_call(
        kern,
        grid=(b, s // QBLOCK),
        in_specs=[
            pl.BlockSpec(memory_space=pltpu.SMEM),
            pl.BlockSpec((1, QBLOCK, WIN_WIDTH), lambda bi, j: (bi, j, 0)),
            pl.BlockSpec((1, LANES, s), lambda bi, j: (bi, 0, 0)),
            pl.BlockSpec((1, s, LANES), lambda bi, j: (bi, 0, 0)),
        ],
        out_specs=pl.BlockSpec((1, QBLOCK, WIN_WIDTH), lambda bi, j: (bi, j, 0)),
        out_shape=jax.ShapeDtypeStruct((b, s, WIN_WIDTH), BF16),
        compiler_params=_params("parallel", "parallel"),
        name="window_attention",
    )(sink, qw, kwt, vw)


def _s5_kernel(u_ref, bb_ref, cc_ref, are_ref, aim_ref, y_ref, st_ref, buf_ref):
    @pl.when(pl.program_id(0) == 0)
    def _():
        st_ref[...] = jnp.zeros_like(st_ref)

    rows = u_ref.shape[0]
    steps = rows // SUBLANES
    ns = S5_BLOCK_STATES
    fwd = (lax.broadcasted_iota(jnp.int32, (rows, 1), 0) % SUBLANES) < (SUBLANES // 2)
    for jb in range(S5_BLOCKS):
        u = u_ref[:, jb * LANES:(jb + 1) * LANES]
        z = jnp.zeros_like(u)
        lhs = jnp.concatenate([jnp.where(fwd, u, z), jnp.where(fwd, z, u)], axis=1)
        buf_ref[...] = jnp.dot(lhs, bb_ref[0, jb], preferred_element_type=F32)
        a_re = are_ref[0, :, jb * ns:(jb + 1) * ns]
        a_im = aim_ref[0, :, jb * ns:(jb + 1) * ns]

        def step(t, carry):
            s_re, s_im = carry
            r0 = pl.multiple_of(t * SUBLANES, SUBLANES)
            n_re = a_re * s_re - a_im * s_im + buf_ref[pl.ds(r0, SUBLANES), 0:ns]
            n_im = a_re * s_im + a_im * s_re + buf_ref[pl.ds(r0, SUBLANES), ns:2 * ns]
            buf_ref[pl.ds(r0, SUBLANES), 0:ns] = n_re
            buf_ref[pl.ds(r0, SUBLANES), ns:2 * ns] = n_im
            return n_re, n_im

        s_re, s_im = lax.fori_loop(
            0, steps, step, (st_ref[0, :, jb * ns:(jb + 1) * ns], st_ref[1, :, jb * ns:(jb + 1) * ns]))
        st_ref[0, :, jb * ns:(jb + 1) * ns] = s_re
        st_ref[1, :, jb * ns:(jb + 1) * ns] = s_im
        y = jnp.dot(buf_ref[...].astype(BF16), cc_ref[0, jb], preferred_element_type=F32)
        y_ref[:, jb * LANES:(jb + 1) * LANES] = jnp.where(fwd, y[:, 0:LANES], y[:, LANES:2 * LANES])


def _s5_scan(l, u8, bbcat, ccat, a_re, a_im):
    rows_total, width = u8.shape
    rows = S5_CHUNK * SUBLANES
    n_states = S5_GROUPS * S5_STATE
    return pl.pallas_call(
        _s5_kernel,
        grid=(rows_total // rows,),
        in_specs=[
            pl.BlockSpec((rows, width), lambda i: (i, 0)),
            _const_spec((1, S5_BLOCKS, 2 * LANES, 2 * S5_BLOCK_STATES), lambda i: (l, 0, 0, 0)),
            _const_spec((1, S5_BLOCKS, 2 * S5_BLOCK_STATES, 2 * LANES), lambda i: (l, 0, 0, 0)),
            _const_spec((1, SUBLANES, n_states), lambda i: (l, 0, 0)),
            _const_spec((1, SUBLANES, n_states), lambda i: (l, 0, 0)),
        ],
        out_specs=pl.BlockSpec((rows, width), lambda i: (i, 0)),
        out_shape=jax.ShapeDtypeStruct((rows_total, width), F32),
        scratch_shapes=[
            pltpu.VMEM((2, SUBLANES, n_states), F32),
            pltpu.VMEM((rows, 2 * S5_BLOCK_STATES), F32),
        ],
        compiler_params=_params("arbitrary"),
        name="s5_scan",
    )(u8, bbcat, ccat, a_re, a_im)


def _gelu_tanh(x):
    return 0.5 * x * (1.0 + jnp.tanh(math.sqrt(2.0 / math.pi) * (x + 0.044715 * (x * x * x))))


def _merge_kernel(h_ref, mod_ref, yd_ref, su_ref, yf_ref, yb_ref, yw_ref, gt_ref,
                  dsk_ref, wglu_ref, wd_ref, ws_ref, ww_ref, wo_ref, o_ref):
    d = h_ref.shape[2]
    y = su_ref[0] * dsk_ref[0] + yf_ref[0] + yb_ref[0]
    g = _gelu_tanh(y)
    ys = g * _sigmoid(jnp.dot(g.astype(BF16), wglu_ref[0], preferred_element_type=F32))
    m = (gt_ref[0, :, 0:d] * jnp.dot(yd_ref[0], wd_ref[0], preferred_element_type=F32)
         + gt_ref[0, :, d:2 * d] * jnp.dot(ys.astype(BF16), ws_ref[0], preferred_element_type=F32)
         + gt_ref[0, :, 2 * d:3 * d] * jnp.dot(yw_ref[0], ww_ref[0], preferred_element_type=F32))
    g1 = mod_ref[0, 0, 2:3]
    o_ref[0] = h_ref[0] + g1 * jnp.dot(m.astype(BF16), wo_ref[0], preferred_element_type=F32)


def _merge(l, h, mods, yd, su, yf, yb, yw, gates, dsk, wglu, wd, ws, ww, wo):
    b, s, d = h.shape
    nt = s // ROW_TILE
    mod_idx = lambda bi, j: (l, jnp.where(j == 0, b, bi), 0, 0)
    tile = lambda width: pl.BlockSpec((1, ROW_TILE, width), lambda bi, j: (bi, j, 0))
    wspec = lambda r, c: _const_spec((1, r, c), lambda bi, j: (l, 0, 0))
    return pl.pallas_call(
        _merge_kernel,
        grid=(b, nt),
        in_specs=[
            tile(d), pl.BlockSpec((1, 1, 6, d), mod_idx),
            tile(DIFF_WIDTH), tile(S5_WIDTH), tile(S5_WIDTH), tile(S5_WIDTH), tile(WIN_WIDTH), tile(3 * d),
            wspec(1, S5_WIDTH), wspec(S5_WIDTH, S5_WIDTH),
            wspec(DIFF_WIDTH, d), wspec(S5_WIDTH, d), wspec(WIN_WIDTH, d), wspec(d, d),
        ],
        out_specs=tile(d),
        out_shape=jax.ShapeDtypeStruct((b, s, d), F32),
        compiler_params=_params("parallel", "parallel"),
        name="merge",
    )(h, mods, yd, su, yf, yb, yw, gates, dsk, wglu, wd, ws, ww, wo)


def _ffn_kernel(h_ref, mod_ref, n2g_ref, w1_ref, w2_ref, o_ref):
    x = h_ref[0]
    m = mod_ref[0, 0]
    f = (_rms(x, n2g_ref[0]) * (1.0 + m[4:5]) + m[3:4]).astype(BF16)
    t = jnp.maximum(jnp.dot(f, w1_ref[0], preferred_element_type=F32), 0.0)
    t = (t * t).astype(BF16)
    o_ref[0] = x + m[5:6] * jnp.dot(t, w2_ref[0], preferred_element_type=F32)


def _ffn(l, h, mods, n2g, w1, w2):
    b, s, d = h.shape
    nt = s // ROW_TILE
    d_ff = w1.shape[-1]
    mod_idx = lambda bi, j: (l, jnp.where(j == 0, b, bi), 0, 0)
    tile = pl.BlockSpec((1, ROW_TILE, d), lambda bi, j: (bi, j, 0))
    return pl.pallas_call(
        _ffn_kernel,
        grid=(b, nt),
        in_specs=[
            tile, pl.BlockSpec((1, 1, 6, d), mod_idx),
            _const_spec((1, 1, d), lambda bi, j: (l, 0, 0)),
            _const_spec((1, d, d_ff), lambda bi, j: (l, 0, 0)),
            _const_spec((1, d_ff, d), lambda bi, j: (l, 0, 0)),
        ],
        out_specs=tile,
        out_shape=jax.ShapeDtypeStruct((b, s, d), F32),
        compiler_params=_params("parallel", "parallel"),
        name="ffn",
    )(h, mods, n2g, w1, w2)


def _inproj_column_order():
    dq = np.arange(512).reshape(2, DIFF_HEADS, DIFF_QK_DIM).transpose(1, 0, 2).reshape(-1)
    dk = 512 + dq
    dv = np.arange(1024, 1536)
    su = np.arange(1536, 2048)
    wq = np.arange(2048, 2560)
    wk = np.arange(2560, 2688)
    wv = np.arange(2688, 2816)
    gt = np.arange(2816, 5888)
    return np.concatenate([dq, dk, wq, wk, dv, wv, su, gt])


def _rope_tables(n_ctx, n_lat):
    n_freq = ROPE_DIM // 4
    inv = ROPE_BASE ** (-jnp.arange(n_freq, dtype=F32) / n_freq)
    rows = n_lat // GRID_W
    r = jnp.repeat(jnp.arange(rows, dtype=F32), GRID_W)
    col = jnp.tile(jnp.arange(GRID_W, dtype=F32), rows)
    ang = jnp.concatenate([r[:, None] * inv, col[:, None] * inv], -1)
    cos, sin = jnp.cos(ang), jnp.sin(ang)
    cosf = jnp.concatenate([jnp.ones((n_ctx, LANES), F32), jnp.tile(cos, (1, 4))], 0)
    sinf = jnp.concatenate([jnp.zeros((n_ctx, LANES), F32),
                            jnp.tile(jnp.concatenate([-sin, sin], -1), (1, 2))], 0)
    return cosf, sinf


def _s5_tables(lam_re, lam_im, log_dt, b_re, b_im, c_re, c_im):
    depth = lam_re.shape[0]
    dt = jnp.exp(log_dt)[..., None]
    mag = jnp.exp(lam_re * dt)
    ang = lam_im * dt
    a_re = mag * jnp.cos(ang)
    a_im = mag * jnp.sin(ang)
    den = jnp.square(lam_re) + jnp.square(lam_im)
    n_re = a_re - 1.0
    f_re = (n_re * lam_re + a_im * lam_im) / den
    f_im = (a_im * lam_re - n_re * lam_im) / den
    bb_re = f_re[..., None] * b_re - f_im[..., None] * b_im
    bb_im = f_re[..., None] * b_im + f_im[..., None] * b_re
    eye = jnp.eye(S5_BLOCK_GROUPS, dtype=F32)
    shp = (depth, 2, S5_BLOCKS, S5_BLOCK_GROUPS, S5_STATE, S5_GROUP)

    def in_block(bb):
        t = jnp.einsum('ldjgph,gk->ldjghkp', bb.reshape(shp), eye)
        return t.reshape(depth, 2, S5_BLOCKS, LANES, S5_BLOCK_STATES).transpose(0, 2, 1, 3, 4)

    bbcat = jnp.concatenate([in_block(bb_re), in_block(bb_im)], -1)
    bbcat = bbcat.reshape(depth, S5_BLOCKS, 2 * LANES, 2 * S5_BLOCK_STATES).astype(BF16)
    shc = (depth, 2, S5_BLOCKS, S5_BLOCK_GROUPS, S5_GROUP, S5_STATE)

    def out_block(c):
        t = jnp.einsum('ldjghp,gk->ljgpdkh', c.reshape(shc), eye)
        return t.reshape(depth, S5_BLOCKS, S5_BLOCK_STATES, 2 * LANES)

    ccat = jnp.concatenate([out_block(c_re), out_block(-c_im)], 2).astype(BF16)
    n_states = S5_GROUPS * S5_STATE
    rep = lambda a: jnp.repeat(a.reshape(depth, 2, n_states), SUBLANES // 2, axis=1)
    return bbcat, ccat, rep(a_re), rep(a_im)


def kernel(x, c, ctx, c_ctx, w_mod, b_mod, norm1_g, norm2_g, w_in, diff_q_norm_g, diff_k_norm_g, diff_lam_q1, diff_lam_k1, diff_lam_q2, diff_lam_k2, diff_out_norm_g, s5_lambda_re, s5_lambda_im, s5_log_dt, s5_b_re, s5_b_im, s5_c_re, s5_c_im, s5_d, s5_w_glu, win_q_norm_g, win_k_norm_g, win_sink, w_proj_diff, w_proj_s5, w_proj_win, w_out, w_ff1, w_ff2):
    b, n_lat, d = x.shape
    n_ctx = ctx.shape[1]
    s = n_ctx + n_lat
    depth = w_in.shape[0]
    assert n_ctx == ROW_TILE and n_lat % ROW_TILE == 0 and b == SUBLANES // 2

    cc = jnp.zeros((SUBLANES, d), F32).at[:b].set(c).at[b].set(c_ctx)
    mods = _modulation(cc, w_mod, b_mod).reshape(depth, SUBLANES, 6, d)

    w_in_p = w_in[:, :, _inproj_column_order()].astype(BF16)
    scale_d = DIFF_QK_DIM ** -0.5
    scale_w = WIN_HEAD_DIM ** -0.5
    gtab = jnp.concatenate([
        jnp.tile(diff_q_norm_g * scale_d, (1, 8)), jnp.tile(diff_k_norm_g, (1, 8)),
        jnp.tile(win_q_norm_g * scale_w, (1, 8)), jnp.tile(win_k_norm_g, (1, 2))], -1)[:, None, :]
    cosf, sinf = _rope_tables(n_ctx, n_lat)
    lam_p = jnp.stack([diff_lam_q1, diff_lam_k1, diff_lam_q2, diff_lam_k2], 1)
    bbcat, ccat, a_re, a_im = _s5_tables(s5_lambda_re, s5_lambda_im, s5_log_dt, s5_b_re, s5_b_im,
                                         s5_c_re, s5_c_im)
    bf = lambda w: w.astype(BF16)
    wglu, wd, ws, ww, wo, w1, w2 = (bf(s5_w_glu), bf(w_proj_diff), bf(w_proj_s5), bf(w_proj_win),
                                    bf(w_out), bf(w_ff1), bf(w_ff2))
    n1g, n2g = norm1_g[:, None, :], norm2_g[:, None, :]
    og, dsk = diff_out_norm_g[:, None, :], s5_d[:, None, :]
    rev = np.concatenate([np.arange(n_ctx - 1, -1, -1), np.arange(s - 1, n_ctx - 1, -1)])

    h = jnp.concatenate([ctx, x], axis=1)
    for l in range(depth):
        lam_init = 0.8 - 0.6 * math.exp(-0.3 * l)
        qd, kdt, qw, kwt, vd, vw, su, gates = _inproj(l, h, mods, n1g, w_in_p, gtab, cosf, sinf)
        yd = _diff_attention(l, lam_init, n_ctx, lam_p, og, qd, kdt, vd)
        yw = _window_attention(l, n_ctx, win_sink, qw, kwt, vw)
        su_t = su.astype(BF16).transpose(1, 0, 2)
        u8 = jnp.stack([su_t, su_t[rev]], axis=1).reshape(s * SUBLANES, S5_WIDTH)
        y8 = _s5_scan(l, u8, bbcat, ccat, a_re, a_im).reshape(s, 2, b, S5_WIDTH)
        yf = y8[:, 0].transpose(1, 0, 2)
        yb = y8[:, 1][rev].transpose(1, 0, 2)
        h = _merge(l, h, mods, yd, su, yf, yb, yw, gates, dsk, wglu, wd, ws, ww, wo)
        h = _ffn(l, h, mods, n2g, w1, w2)
    return h[:, n_ctx:, :]
```

```python
import functools
import math

import numpy as np
import jax
import jax.numpy as jnp
from jax import lax
from jax.experimental import pallas as pl
from jax.experimental.pallas import tpu as pltpu

F32 = jnp.float32
BF16 = jnp.bfloat16

GRID_W = 64
QBLOCK = 128
DIFF_HEADS = 4
DIFF_QK_DIM = 64
DIFF_V_DIM = 128
DIFF_WIDTH = DIFF_HEADS * DIFF_V_DIM
S5_WIDTH = 512
S5_GROUP = 16
S5_GROUPS = 32
S5_STATE = 64
WIN_Q_HEADS = 8
WIN_KV_HEADS = 2
WIN_HEAD_DIM = 64
WIN_GROUP = WIN_Q_HEADS // WIN_KV_HEADS
WIN_WIDTH = WIN_Q_HEADS * WIN_HEAD_DIM
WINDOW = 128
ROPE_DIM = 64
ROPE_BASE = 10000.0
EPS = 1e-6
NEG_INF = -1e30

LANES = 128
SUBLANES = 8
ROW_TILE = 256
S5_CHUNK = 128
S5_BLOCK_GROUPS = LANES // S5_GROUP
S5_BLOCKS = S5_GROUPS // S5_BLOCK_GROUPS
S5_BLOCK_STATES = S5_BLOCK_GROUPS * S5_STATE
S5_PITCH = S5_CHUNK + SUBLANES
LOG2E = math.log2(math.e)
VMEM_LIMIT = 48 * 1024 * 1024

N_NORM_CHUNKS = 13
OFF_QD, OFF_KD, OFF_QW, OFF_KW = 0, 512, 1024, 1536
OFF_VD, OFF_VW, OFF_SU, OFF_GT = 1664, 2176, 2304, 2816


def _sigmoid(x):
    return 1.0 / (1.0 + jnp.exp(-x))


def _rms(x, g):
    return x * lax.rsqrt(jnp.mean(x * x, axis=-1, keepdims=True) + EPS) * g


def _const_spec(shape, index_map):
    return pl.BlockSpec(shape, index_map, pipeline_mode=pl.Buffered(1))


def _params(*sem):
    return pltpu.CompilerParams(dimension_semantics=sem, vmem_limit_bytes=VMEM_LIMIT)


def _mod_kernel(cc_ref, w_ref, b_ref, o_ref):
    cc = cc_ref[...]
    s = cc * _sigmoid(cc)
    o_ref[0] = jnp.dot(s.astype(BF16), w_ref[0].astype(BF16), preferred_element_type=F32) + b_ref[0]


def _modulation(cc, w_mod, b_mod):
    depth, d, d6 = w_mod.shape
    tn = 1536
    return pl.pallas_call(
        _mod_kernel,
        grid=(depth, d6 // tn),
        in_specs=[
            pl.BlockSpec((SUBLANES, d), lambda l, n: (0, 0)),
            pl.BlockSpec((1, d, tn), lambda l, n: (l, 0, n)),
            pl.BlockSpec((1, 1, tn), lambda l, n: (l, 0, n)),
        ],
        out_specs=pl.BlockSpec((1, SUBLANES, tn), lambda l, n: (l, 0, n)),
        out_shape=jax.ShapeDtypeStruct((depth, SUBLANES, d6), F32),
        compiler_params=_params("parallel", "parallel"),
        name="modulation",
    )(cc, w_mod, b_mod.reshape(depth, 1, d6))


def _inproj_kernel(h_ref, mod_ref, n1g_ref, w_ref, gtab_ref, cos_ref, sin_ref,
                   qd_ref, kdt_ref, qw_ref, kwt_ref, vd_ref, vw_ref, su_ref, gt_ref):
    x = h_ref[0]
    m = mod_ref[0, 0]
    a = (_rms(x, n1g_ref[0]) * (1.0 + m[1:2]) + m[0:1]).astype(BF16)
    tm = x.shape[0]
    lane = lax.broadcasted_iota(jnp.int32, (tm, LANES), 1)
    low_seg = lane < ROPE_DIM
    first_half = (lane % ROPE_DIM) < (ROPE_DIM // 2)
    cosf = cos_ref[...]
    sinf = sin_ref[...]
    for c in range(N_NORM_CHUNKS):
        xc = jnp.dot(a, w_ref[0, :, c * LANES:(c + 1) * LANES], preferred_element_type=F32)
        y = xc * xc
        ss_lo = jnp.sum(jnp.where(low_seg, y, 0.0), axis=-1, keepdims=True)
        ss_hi = jnp.sum(jnp.where(low_seg, 0.0, y), axis=-1, keepdims=True)
        r = jnp.where(low_seg, lax.rsqrt(ss_lo / ROPE_DIM + EPS), lax.rsqrt(ss_hi / ROPE_DIM + EPS))
        xn = xc * r * gtab_ref[0, :, c * LANES:(c + 1) * LANES]
        partner = jnp.where(first_half, pltpu.roll(xn, LANES - ROPE_DIM // 2, 1),
                            pltpu.roll(xn, ROPE_DIM // 2, 1))
        out = xn * cosf + partner * sinf
        if c < 4:
            qd_ref[0, :, c * LANES:(c + 1) * LANES] = out.astype(BF16)
        elif c < 8:
            kdt_ref[0, (c - 4) * LANES:(c - 3) * LANES, :] = out.T.astype(BF16)
        elif c < 12:
            qw_ref[0, :, (c - 8) * LANES:(c - 7) * LANES] = out.astype(BF16)
        else:
            kwt_ref[0] = out.T.astype(BF16)
    vd_ref[0] = jnp.dot(a, w_ref[0, :, OFF_VD:OFF_VW], preferred_element_type=F32).astype(BF16)
    vw_ref[0] = jnp.dot(a, w_ref[0, :, OFF_VW:OFF_SU], preferred_element_type=F32).astype(BF16)
    su_ref[0] = jnp.dot(a, w_ref[0, :, OFF_SU:OFF_GT], preferred_element_type=F32)
    gt_ref[0] = _sigmoid(jnp.dot(a, w_ref[0, :, OFF_GT:], preferred_element_type=F32))


def _inproj(l, h, mods, n1g, w_in_p, gtab, cosf, sinf):
    b, s, d = h.shape
    d_in = w_in_p.shape[-1]
    nt = s // ROW_TILE
    mod_idx = lambda bi, j: (l, jnp.where(j == 0, b, bi), 0, 0)
    tile = lambda width: pl.BlockSpec((1, ROW_TILE, width), lambda bi, j: (bi, j, 0))
    out_shapes = (
        jax.ShapeDtypeStruct((b, s, 512), BF16),
        jax.ShapeDtypeStruct((b, 512, s), BF16),
        jax.ShapeDtypeStruct((b, s, WIN_WIDTH), BF16),
        jax.ShapeDtypeStruct((b, LANES, s), BF16),
        jax.ShapeDtypeStruct((b, s, DIFF_WIDTH), BF16),
        jax.ShapeDtypeStruct((b, s, LANES), BF16),
        jax.ShapeDtypeStruct((b, s, S5_WIDTH), F32),
        jax.ShapeDtypeStruct((b, s, 3 * d), F32),
    )
    out_specs = (
        tile(512),
        pl.BlockSpec((1, 512, ROW_TILE), lambda bi, j: (bi, 0, j)),
        tile(WIN_WIDTH),
        pl.BlockSpec((1, LANES, ROW_TILE), lambda bi, j: (bi, 0, j)),
        tile(DIFF_WIDTH), tile(LANES), tile(S5_WIDTH), tile(3 * d),
    )
    return pl.pallas_call(
        _inproj_kernel,
        grid=(b, nt),
        in_specs=[
            tile(d),
            pl.BlockSpec((1, 1, 6, d), mod_idx),
            _const_spec((1, 1, d), lambda bi, j: (l, 0, 0)),
            _const_spec((1, d, d_in), lambda bi, j: (l, 0, 0)),
            _const_spec((1, 1, N_NORM_CHUNKS * LANES), lambda bi, j: (l, 0, 0)),
            pl.BlockSpec((ROW_TILE, LANES), lambda bi, j: (j, 0)),
            pl.BlockSpec((ROW_TILE, LANES), lambda bi, j: (j, 0)),
        ],
        out_specs=out_specs,
        out_shape=out_shapes,
        compiler_params=_params("parallel", "parallel"),
        name="inproj",
    )(h, mods, n1g, w_in_p, gtab, cosf, sinf)


def _diff_kernel(lam_ref, og_ref, q_ref, kt_ref, v_ref, o_ref, *, lam_init, n_ctx):
    lp = lam_ref[0]
    lam = (jnp.exp(jnp.sum(lp[0:1] * lp[1:2], axis=-1, keepdims=True))
           - jnp.exp(jnp.sum(lp[2:3] * lp[3:4], axis=-1, keepdims=True)) + lam_init)
    q = q_ref[0]
    og = og_ref[0]

    def attend(n_keys):
        outs = []
        for m in range(2):
            qm = q[:, m * DIFF_QK_DIM:(m + 1) * DIFF_QK_DIM]
            km = kt_ref[0, m * DIFF_QK_DIM:(m + 1) * DIFF_QK_DIM, 0:n_keys]
            s = jnp.dot(qm, km, preferred_element_type=F32)
            e = jnp.exp2(s - jnp.max(s, axis=-1, keepdims=True))
            den = jnp.sum(e, axis=-1, keepdims=True)
            outs.append(jnp.dot(e.astype(BF16), v_ref[0, 0:n_keys, :], preferred_element_type=F32) / den)
        o = outs[0] - lam * outs[1]
        o_ref[0] = (_rms(o, og) * (1.0 - lam_init)).astype(o_ref.dtype)

    is_ctx = pl.program_id(2) == 0

    @pl.when(is_ctx)
    def _():
        attend(n_ctx)

    @pl.when(jnp.logical_not(is_ctx))
    def _():
        attend(kt_ref.shape[2])


def _diff_attention(l, lam_init, n_ctx, lam_p, out_g, qd, kdt, vd):
    b, s, _ = qd.shape
    nt = s // ROW_TILE
    kern = functools.partial(_diff_kernel, lam_init=lam_init, n_ctx=n_ctx)
    return pl.pallas_call(
        kern,
        grid=(b, DIFF_HEADS, nt),
        in_specs=[
            _const_spec((1, 4, DIFF_QK_DIM), lambda bi, hd, j: (l, 0, 0)),
            _const_spec((1, 1, DIFF_V_DIM), lambda bi, hd, j: (l, 0, 0)),
            pl.BlockSpec((1, ROW_TILE, LANES), lambda bi, hd, j: (bi, j, hd)),
            pl.BlockSpec((1, LANES, s), lambda bi, hd, j: (bi, hd, 0)),
            pl.BlockSpec((1, s, DIFF_V_DIM), lambda bi, hd, j: (bi, 0, hd)),
        ],
        out_specs=pl.BlockSpec((1, ROW_TILE, DIFF_V_DIM), lambda bi, hd, j: (bi, j, hd)),
        out_shape=jax.ShapeDtypeStruct((b, s, DIFF_WIDTH), BF16),
        compiler_params=_params("parallel", "parallel", "parallel"),
        name="diff_attention",
    )(lam_p, out_g, qd, kdt, vd)


def _win_kernel(sink_ref, q_ref, kt_ref, v_ref, o_ref, *, layer, n_ctx):
    j = pl.program_id(1)
    s_total = kt_ref.shape[2]
    band = 3 * QBLOCK
    q = q_ref[0]
    rows = WIN_GROUP * QBLOCK
    row_group = lax.broadcasted_iota(jnp.int32, (rows, 1), 0) // QBLOCK

    def run(with_band):
        if with_band:
            start = pl.multiple_of(QBLOCK * jnp.minimum(j - 1, (s_total - band) // QBLOCK), QBLOCK)
            qpos = QBLOCK * j + lax.broadcasted_iota(jnp.int32, (QBLOCK, band), 0)
            kpos = start + lax.broadcasted_iota(jnp.int32, (QBLOCK, band), 1)
            valid = jnp.where(kpos >= n_ctx, jnp.abs(kpos - qpos), WINDOW + 1) <= WINDOW
            bias = jnp.concatenate([jnp.where(valid, 0.0, NEG_INF)] * WIN_GROUP, axis=0)
        for kv in range(WIN_KV_HEADS):
            q4 = jnp.concatenate(
                [q[:, (kv * WIN_GROUP + g) * WIN_HEAD_DIM:(kv * WIN_GROUP + g + 1) * WIN_HEAD_DIM]
                 for g in range(WIN_GROUP)], axis=0)
            sink = jnp.zeros((rows, 1), F32)
            for g in range(WIN_GROUP):
                sink = jnp.where(row_group == g, sink_ref[layer, kv * WIN_GROUP + g] * LOG2E, sink)
            ksl = slice(kv * WIN_HEAD_DIM, (kv + 1) * WIN_HEAD_DIM)
            s_c = jnp.dot(q4, kt_ref[0, ksl, 0:n_ctx], preferred_element_type=F32)
            mx = jnp.maximum(jnp.max(s_c, axis=-1, keepdims=True), sink)
            if with_band:
                s_b = jnp.dot(q4, kt_ref[0, ksl, pl.ds(start, band)], preferred_element_type=F32) + bias
                mx = jnp.maximum(mx, jnp.max(s_b, axis=-1, keepdims=True))
            e_c = jnp.exp2(s_c - mx)
            den = jnp.sum(e_c, axis=-1, keepdims=True) + jnp.exp2(sink - mx)
            o = jnp.dot(e_c.astype(BF16), v_ref[0, 0:n_ctx, ksl], preferred_element_type=F32)
            if with_band:
                e_b = jnp.exp2(s_b - mx)
                den = den + jnp.sum(e_b, axis=-1, keepdims=True)
                o = o + jnp.dot(e_b.astype(BF16), v_ref[0, pl.ds(start, band), ksl],
                                preferred_element_type=F32)
            o = o / den
            for g in range(WIN_GROUP):
                c0 = (kv * WIN_GROUP + g) * WIN_HEAD_DIM
                o_ref[0, :, c0:c0 + WIN_HEAD_DIM] = o[g * QBLOCK:(g + 1) * QBLOCK].astype(o_ref.dtype)

    is_ctx = j < n_ctx // QBLOCK

    @pl.when(is_ctx)
    def _():
        run(False)

    @pl.when(jnp.logical_not(is_ctx))
    def _():
        run(True)


def _window_attention(l, n_ctx, sink, qw, kwt, vw):
    b, s, _ = qw.shape
    kern = functools.partial(_win_kernel, layer=l, n_ctx=n_ctx)
    return pl.pallas_call(
        kern,
        grid=(b, s // QBLOCK),
        in_specs=[
            pl.BlockSpec(memory_space=pltpu.SMEM),
            pl.BlockSpec((1, QBLOCK, WIN_WIDTH), lambda bi, j: (bi, j, 0)),
            pl.BlockSpec((1, LANES, s), lambda bi, j: (bi, 0, 0)),
            pl.BlockSpec((1, s, LANES), lambda bi, j: (bi, 0, 0)),
        ],
        out_specs=pl.BlockSpec((1, QBLOCK, WIN_WIDTH), lambda bi, j: (bi, j, 0)),
        out_shape=jax.ShapeDtypeStruct((b, s, WIN_WIDTH), BF16),
        compiler_params=_params("parallel", "parallel"),
        name="window_attention",
    )(sink, qw, kwt, vw)


def _s5_kernel(uf_ref, ub_ref, bb_ref, cc_ref, are_ref, aim_ref, yf_ref, yb_ref,
               st_ref, slab_ref, lhs_ref, buf_ref, y8_ref, of_ref, ob_ref):
    @pl.when(pl.program_id(0) == 0)
    def _():
        st_ref[...] = jnp.zeros_like(st_ref)

    nb, steps, width = uf_ref.shape
    rows = steps * SUBLANES
    ns = S5_BLOCK_STATES
    for k in range(nb):
        for jb in range(S5_BLOCKS):
            lanes = slice(jb * LANES, (jb + 1) * LANES)
            slab_ref[jb, k * S5_PITCH:k * S5_PITCH + steps, :] = uf_ref[k, :, lanes]
            slab_ref[jb, (nb + k) * S5_PITCH:(nb + k) * S5_PITCH + steps, :] = ub_ref[k, :, lanes]
    fwd_sub = lax.broadcasted_iota(jnp.int32, (SUBLANES, LANES), 0) < nb

    def gather(i, carry):
        r0 = pl.multiple_of(i * SUBLANES, SUBLANES)
        for jb in range(S5_BLOCKS):
            a = slab_ref[jb, pl.ds(i, SUBLANES, stride=S5_PITCH), :]
            b = slab_ref[jb, pl.ds(steps - 1 - i, SUBLANES, stride=S5_PITCH), :]
            lhs_ref[pl.ds(r0, SUBLANES), jb * LANES:(jb + 1) * LANES] = jnp.where(fwd_sub, a, b)
        return carry

    lax.fori_loop(0, steps, gather, 0)

    fwd = (lax.broadcasted_iota(jnp.int32, (rows, 1), 0) % SUBLANES) < nb
    for jb in range(S5_BLOCKS):
        u = lhs_ref[:, jb * LANES:(jb + 1) * LANES].astype(BF16)
        z = jnp.zeros_like(u)
        lhs = jnp.concatenate([jnp.where(fwd, u, z), jnp.where(fwd, z, u)], axis=1)
        buf_ref[...] = jnp.dot(lhs, bb_ref[0, jb], preferred_element_type=F32)
        a_re = are_ref[0, :, jb * ns:(jb + 1) * ns]
        a_im = aim_ref[0, :, jb * ns:(jb + 1) * ns]

        def step(t, carry):
            s_re, s_im = carry
            r0 = pl.multiple_of(t * SUBLANES, SUBLANES)
            n_re = a_re * s_re - a_im * s_im + buf_ref[pl.ds(r0, SUBLANES), 0:ns]
            n_im = a_re * s_im + a_im * s_re + buf_ref[pl.ds(r0, SUBLANES), ns:2 * ns]
            buf_ref[pl.ds(r0, SUBLANES), 0:ns] = n_re
            buf_ref[pl.ds(r0, SUBLANES), ns:2 * ns] = n_im
            return n_re, n_im

        s_re, s_im = lax.fori_loop(
            0, steps, step, (st_ref[0, :, jb * ns:(jb + 1) * ns], st_ref[1, :, jb * ns:(jb + 1) * ns]),
            unroll=2)
        st_ref[0, :, jb * ns:(jb + 1) * ns] = s_re
        st_ref[1, :, jb * ns:(jb + 1) * ns] = s_im
        y = jnp.dot(buf_ref[...].astype(BF16), cc_ref[0, jb], preferred_element_type=F32)
        y8_ref[:, jb * LANES:(jb + 1) * LANES] = jnp.where(fwd, y[:, 0:LANES], y[:, LANES:2 * LANES])

    def scatter(i, carry):
        r0 = pl.multiple_of(i * SUBLANES, SUBLANES)
        for jb in range(S5_BLOCKS):
            v = y8_ref[pl.ds(r0, SUBLANES), jb * LANES:(jb + 1) * LANES]
            of_ref[jb, pl.ds(i, SUBLANES, stride=S5_PITCH), :] = v
            ob_ref[jb, pl.ds(steps - 1 - i, SUBLANES, stride=S5_PITCH), :] = v
        return carry

    lax.fori_loop(0, steps, scatter, 0)
    for k in range(nb):
        for jb in range(S5_BLOCKS):
            lanes = slice(jb * LANES, (jb + 1) * LANES)
            yf_ref[k, :, lanes] = of_ref[jb, k * S5_PITCH:k * S5_PITCH + steps, :]
            yb_ref[k, :, lanes] = ob_ref[jb, (nb + k) * S5_PITCH:(nb + k) * S5_PITCH + steps, :]


def _s5_scan(l, n_ctx, su, bbcat, ccat, a_re, a_im):
    b, s, width = su.shape
    rows = S5_CHUNK * SUBLANES
    n_states = S5_GROUPS * S5_STATE
    n_tiles = s // S5_CHUNK
    ctx_tiles = n_ctx // S5_CHUNK
    mirror = lambda c: jnp.where(c < ctx_tiles, ctx_tiles - 1 - c, n_tiles + ctx_tiles - 1 - c)
    fwd_spec = pl.BlockSpec((b, S5_CHUNK, width), lambda c: (0, c, 0))
    bwd_spec = pl.BlockSpec((b, S5_CHUNK, width), lambda c: (0, mirror(c), 0))
    slab = pltpu.VMEM((width // LANES, SUBLANES * S5_PITCH, LANES), F32)
    return pl.pallas_call(
        _s5_kernel,
        grid=(n_tiles,),
        in_specs=[
            fwd_spec, bwd_spec,
            _const_spec((1, S5_BLOCKS, 2 * LANES, 2 * S5_BLOCK_STATES), lambda c: (l, 0, 0, 0)),
            _const_spec((1, S5_BLOCKS, 2 * S5_BLOCK_STATES, 2 * LANES), lambda c: (l, 0, 0, 0)),
            _const_spec((1, SUBLANES, n_states), lambda c: (l, 0, 0)),
            _const_spec((1, SUBLANES, n_states), lambda c: (l, 0, 0)),
        ],
        out_specs=(fwd_spec, bwd_spec),
        out_shape=(jax.ShapeDtypeStruct((b, s, width), F32), jax.ShapeDtypeStruct((b, s, width), F32)),
        scratch_shapes=[
            pltpu.VMEM((2, SUBLANES, n_states), F32),
            slab,
            pltpu.VMEM((rows, width), F32),
            pltpu.VMEM((rows, 2 * S5_BLOCK_STATES), F32),
            pltpu.VMEM((rows, width), F32),
            slab, slab,
        ],
        compiler_params=_params("arbitrary"),
        name="s5_scan",
    )(su, su, bbcat, ccat, a_re, a_im)


def _gelu_tanh(x):
    return 0.5 * x * (1.0 + jnp.tanh(math.sqrt(2.0 / math.pi) * (x + 0.044715 * (x * x * x))))


def _merge_kernel(h_ref, mod_ref, yd_ref, su_ref, yf_ref, yb_ref, yw_ref, gt_ref,
                  dsk_ref, wglu_ref, wd_ref, ws_ref, ww_ref, wo_ref, o_ref):
    d = h_ref.shape[2]
    y = su_ref[0] * dsk_ref[0] + yf_ref[0] + yb_ref[0]
    g = _gelu_tanh(y)
    ys = g * _sigmoid(jnp.dot(g.astype(BF16), wglu_ref[0], preferred_element_type=F32))
    m = (gt_ref[0, :, 0:d] * jnp.dot(yd_ref[0], wd_ref[0], preferred_element_type=F32)
         + gt_ref[0, :, d:2 * d] * jnp.dot(ys.astype(BF16), ws_ref[0], preferred_element_type=F32)
         + gt_ref[0, :, 2 * d:3 * d] * jnp.dot(yw_ref[0], ww_ref[0], preferred_element_type=F32))
    g1 = mod_ref[0, 0, 2:3]
    o_ref[0] = h_ref[0] + g1 * jnp.dot(m.astype(BF16), wo_ref[0], preferred_element_type=F32)


def _merge(l, h, mods, yd, su, yf, yb, yw, gates, dsk, wglu, wd, ws, ww, wo):
    b, s, d = h.shape
    nt = s // ROW_TILE
    mod_idx = lambda bi, j: (l, jnp.where(j == 0, b, bi), 0, 0)
    tile = lambda width: pl.BlockSpec((1, ROW_TILE, width), lambda bi, j: (bi, j, 0))
    wspec = lambda r, c: _const_spec((1, r, c), lambda bi, j: (l, 0, 0))
    return pl.pallas_call(
        _merge_kernel,
        grid=(b, nt),
        in_specs=[
            tile(d), pl.BlockSpec((1, 1, 6, d), mod_idx),
            tile(DIFF_WIDTH), tile(S5_WIDTH), tile(S5_WIDTH), tile(S5_WIDTH), tile(WIN_WIDTH), tile(3 * d),
            wspec(1, S5_WIDTH), wspec(S5_WIDTH, S5_WIDTH),
            wspec(DIFF_WIDTH, d), wspec(S5_WIDTH, d), wspec(WIN_WIDTH, d), wspec(d, d),
        ],
        out_specs=tile(d),
        out_shape=jax.ShapeDtypeStruct((b, s, d), F32),
        compiler_params=_params("parallel", "parallel"),
        name="merge",
    )(h, mods, yd, su, yf, yb, yw, gates, dsk, wglu, wd, ws, ww, wo)


def _ffn_kernel(h_ref, mod_ref, n2g_ref, w1_ref, w2_ref, o_ref, *, latent_only):
    def run():
        x = h_ref[0]
        m = mod_ref[0, 0]
        f = (_rms(x, n2g_ref[0]) * (1.0 + m[4:5]) + m[3:4]).astype(BF16)
        t = jnp.maximum(jnp.dot(f, w1_ref[0], preferred_element_type=F32), 0.0)
        t = (t * t).astype(BF16)
        o_ref[0] = x + m[5:6] * jnp.dot(t, w2_ref[0], preferred_element_type=F32)

    if latent_only:
        pl.when(pl.program_id(1) > 0)(run)
    else:
        run()


def _ffn(l, h, mods, n2g, w1, w2, latent_only):
    b, s, d = h.shape
    nt = s // ROW_TILE
    d_ff = w1.shape[-1]
    mod_idx = lambda bi, j: (l, jnp.where(j == 0, b, bi), 0, 0)
    tile = pl.BlockSpec((1, ROW_TILE, d), lambda bi, j: (bi, j, 0))
    if latent_only:
        out_spec = pl.BlockSpec((1, ROW_TILE, d), lambda bi, j: (bi, jnp.maximum(j - 1, 0), 0))
        out_rows = s - ROW_TILE
    else:
        out_spec, out_rows = tile, s
    return pl.pallas_call(
        functools.partial(_ffn_kernel, latent_only=latent_only),
        grid=(b, nt),
        in_specs=[
            tile, pl.BlockSpec((1, 1, 6, d), mod_idx),
            _const_spec((1, 1, d), lambda bi, j: (l, 0, 0)),
            _const_spec((1, d, d_ff), lambda bi, j: (l, 0, 0)),
            _const_spec((1, d_ff, d), lambda bi, j: (l, 0, 0)),
        ],
        out_specs=out_spec,
        out_shape=jax.ShapeDtypeStruct((b, out_rows, d), F32),
        compiler_params=_params("parallel", "arbitrary"),
        name="ffn",
    )(h, mods, n2g, w1, w2)


def _permute_inproj(w_in):
    depth, d, _ = w_in.shape
    qk = w_in[:, :, 0:1024].reshape(depth, d, 2, 2, DIFF_HEADS, DIFF_QK_DIM)
    qk = qk.transpose(0, 1, 2, 4, 3, 5).reshape(depth, d, 1024)
    parts = [qk, w_in[:, :, 2048:2688], w_in[:, :, 1024:1536], w_in[:, :, 2688:2816],
             w_in[:, :, 1536:2048], w_in[:, :, 2816:]]
    return jnp.concatenate(parts, -1).astype(BF16)


def _rope_tables(n_ctx, n_lat):
    n_freq = ROPE_DIM // 4
    inv = ROPE_BASE ** (-jnp.arange(n_freq, dtype=F32) / n_freq)
    rows = n_lat // GRID_W
    r = jnp.repeat(jnp.arange(rows, dtype=F32), GRID_W)
    col = jnp.tile(jnp.arange(GRID_W, dtype=F32), rows)
    ang = jnp.concatenate([r[:, None] * inv, col[:, None] * inv], -1)
    cos, sin = jnp.cos(ang), jnp.sin(ang)
    cosf = jnp.concatenate([jnp.ones((n_ctx, LANES), F32), jnp.tile(cos, (1, 4))], 0)
    sinf = jnp.concatenate([jnp.zeros((n_ctx, LANES), F32),
                            jnp.tile(jnp.concatenate([-sin, sin], -1), (1, 2))], 0)
    return cosf, sinf


def _s5_disc_kernel(lre_ref, lim_ref, ldt_ref, bre_ref, bim_ref, are_ref, aim_ref, bbre_ref, bbim_ref):
    lre = lre_ref[...]
    lim = lim_ref[...]
    dt = jnp.exp(ldt_ref[...])
    mag = jnp.exp(lre * dt)
    ang = lim * dt
    a_re = mag * jnp.cos(ang)
    a_im = mag * jnp.sin(ang)
    den = lre * lre + lim * lim
    n_re = a_re - 1.0
    f_re = (n_re * lre + a_im * lim) / den
    f_im = (a_im * lre - n_re * lim) / den
    are_ref[...] = a_re
    aim_ref[...] = a_im
    b_re = bre_ref[...]
    b_im = bim_ref[...]
    bbre_ref[...] = f_re * b_re - f_im * b_im
    bbim_ref[...] = f_re * b_im + f_im * b_re


def _s5_discretize(lam_re, lam_im, log_dt, b_re, b_im):
    depth, two, g, p = lam_re.shape
    hg = b_re.shape[-1]
    r = depth * two * g
    row = lambda a: a.reshape(r, 1, p)
    bt = lambda a: a.reshape(r, p, hg).transpose(0, 2, 1)
    vec = jax.ShapeDtypeStruct((r, 1, p), F32)
    mat = jax.ShapeDtypeStruct((r, hg, p), F32)
    a_re, a_im, bb_re, bb_im = pl.pallas_call(
        _s5_disc_kernel, out_shape=(vec, vec, mat, mat), name="s5_discretize",
    )(row(lam_re), row(lam_im), log_dt.reshape(r, 1, 1), bt(b_re), bt(b_im))
    shape = (depth, two, g, hg, p)
    return (a_re.reshape(depth, two, g, p), a_im.reshape(depth, two, g, p),
            bb_re.reshape(shape), bb_im.reshape(shape))


def _s5_tables(lam_re, lam_im, log_dt, b_re, b_im, c_re, c_im):
    depth = lam_re.shape[0]
    a_re, a_im, bb_re, bb_im = _s5_discretize(lam_re, lam_im, log_dt, b_re, b_im)
    bb_re = bb_re.transpose(0, 1, 2, 4, 3)
    bb_im = bb_im.transpose(0, 1, 2, 4, 3)
    eye = jnp.eye(S5_BLOCK_GROUPS, dtype=F32)
    shp = (depth, 2, S5_BLOCKS, S5_BLOCK_GROUPS, S5_STATE, S5_GROUP)

    def in_block(bb):
        t = jnp.einsum('ldjgph,gk->ldjghkp', bb.reshape(shp), eye)
        return t.reshape(depth, 2, S5_BLOCKS, LANES, S5_BLOCK_STATES).transpose(0, 2, 1, 3, 4)

    bbcat = jnp.concatenate([in_block(bb_re), in_block(bb_im)], -1)
    bbcat = bbcat.reshape(depth, S5_BLOCKS, 2 * LANES, 2 * S5_BLOCK_STATES).astype(BF16)
    shc = (depth, 2, S5_BLOCKS, S5_BLOCK_GROUPS, S5_GROUP, S5_STATE)

    def out_block(c):
        t = jnp.einsum('ldjghp,gk->ljgpdkh', c.reshape(shc), eye)
        return t.reshape(depth, S5_BLOCKS, S5_BLOCK_STATES, 2 * LANES)

    ccat = jnp.concatenate([out_block(c_re), out_block(-c_im)], 2).astype(BF16)
    n_states = S5_GROUPS * S5_STATE
    rep = lambda a: jnp.repeat(a.reshape(depth, 2, n_states), SUBLANES // 2, axis=1)
    return bbcat, ccat, rep(a_re), rep(a_im)


def kernel(x, c, ctx, c_ctx, w_mod, b_mod, norm1_g, norm2_g, w_in, diff_q_norm_g, diff_k_norm_g, diff_lam_q1, diff_lam_k1, diff_lam_q2, diff_lam_k2, diff_out_norm_g, s5_lambda_re, s5_lambda_im, s5_log_dt, s5_b_re, s5_b_im, s5_c_re, s5_c_im, s5_d, s5_w_glu, win_q_norm_g, win_k_norm_g, win_sink, w_proj_diff, w_proj_s5, w_proj_win, w_out, w_ff1, w_ff2):
    b, n_lat, d = x.shape
    n_ctx = ctx.shape[1]
    s = n_ctx + n_lat
    depth = w_in.shape[0]
    assert n_ctx == ROW_TILE and n_lat % ROW_TILE == 0 and b == SUBLANES // 2

    cc = jnp.zeros((SUBLANES, d), F32).at[:b].set(c).at[b].set(c_ctx)
    mods = _modulation(cc, w_mod, b_mod).reshape(depth, SUBLANES, 6, d)

    w_in_p = _permute_inproj(w_in)
    scale_d = DIFF_QK_DIM ** -0.5 * LOG2E
    scale_w = WIN_HEAD_DIM ** -0.5 * LOG2E
    gtab = jnp.concatenate([
        jnp.tile(diff_q_norm_g * scale_d, (1, 8)), jnp.tile(diff_k_norm_g, (1, 8)),
        jnp.tile(win_q_norm_g * scale_w, (1, 8)), jnp.tile(win_k_norm_g, (1, 2))], -1)[:, None, :]
    cosf, sinf = _rope_tables(n_ctx, n_lat)
    lam_p = jnp.stack([diff_lam_q1, diff_lam_k1, diff_lam_q2, diff_lam_k2], 1)
    bbcat, ccat, a_re, a_im = _s5_tables(s5_lambda_re, s5_lambda_im, s5_log_dt, s5_b_re, s5_b_im,
                                         s5_c_re, s5_c_im)
    bf = lambda w: w.astype(BF16)
    wglu, wd, ws, ww, wo, w1, w2 = (bf(s5_w_glu), bf(w_proj_diff), bf(w_proj_s5), bf(w_proj_win),
                                    bf(w_out), bf(w_ff1), bf(w_ff2))
    n1g, n2g = norm1_g[:, None, :], norm2_g[:, None, :]
    og, dsk = diff_out_norm_g[:, None, :], s5_d[:, None, :]

    h = jnp.concatenate([ctx, x], axis=1)
    for l in range(depth):
        lam_init = 0.8 - 0.6 * math.exp(-0.3 * l)
        qd, kdt, qw, kwt, vd, vw, su, gates = _inproj(l, h, mods, n1g, w_in_p, gtab, cosf, sinf)
        yd = _diff_attention(l, lam_init, n_ctx, lam_p, og, qd, kdt, vd)
        yw = _window_attention(l, n_ctx, win_sink, qw, kwt, vw)
        yf, yb = _s5_scan(l, n_ctx, su, bbcat, ccat, a_re, a_im)
        h = _merge(l, h, mods, yd, su, yf, yb, yw, gates, dsk, wglu, wd, ws, ww, wo)
        h = _ffn(l, h, mods, n2g, w1, w2, latent_only=(l == depth - 1))
    return h
```

```python
import functools
import math

import numpy as np
import jax
import jax.numpy as jnp
from jax import lax
from jax.experimental import pallas as pl
from jax.experimental.pallas import tpu as pltpu

F32 = jnp.float32
BF16 = jnp.bfloat16

GRID_W = 64
QBLOCK = 128
DIFF_HEADS = 4
DIFF_QK_DIM = 64
DIFF_V_DIM = 128
DIFF_WIDTH = DIFF_HEADS * DIFF_V_DIM
S5_WIDTH = 512
S5_GROUP = 16
S5_GROUPS = 32
S5_STATE = 64
WIN_Q_HEADS = 8
WIN_KV_HEADS = 2
WIN_HEAD_DIM = 64
WIN_GROUP = WIN_Q_HEADS // WIN_KV_HEADS
WIN_WIDTH = WIN_Q_HEADS * WIN_HEAD_DIM
WINDOW = 128
ROPE_DIM = 64
ROPE_BASE = 10000.0
EPS = 1e-6
NEG_INF = -1e30

LANES = 128
SUBLANES = 8
ROW_TILE = 256
DIFF_HEADS_PER_STEP = 4
S5_CHUNK = 128
S5_BLOCK_GROUPS = LANES // S5_GROUP
S5_BLOCKS = S5_GROUPS // S5_BLOCK_GROUPS
S5_BLOCK_STATES = S5_BLOCK_GROUPS * S5_STATE
S5_PITCH = S5_CHUNK + SUBLANES
LOG2E = math.log2(math.e)
VMEM_LIMIT = 48 * 1024 * 1024

N_NORM_CHUNKS = 13
OFF_QD, OFF_KD, OFF_QW, OFF_KW = 0, 512, 1024, 1536
OFF_VW, OFF_VD, OFF_SU, OFF_GT = 1664, 1792, 2304, 2816


def _sigmoid(x):
    return 1.0 / (1.0 + jnp.exp(-x))


def _rms(x, g):
    return x * lax.rsqrt(jnp.mean(x * x, axis=-1, keepdims=True) + EPS) * g


def _const_spec(shape, index_map):
    return pl.BlockSpec(shape, index_map, pipeline_mode=pl.Buffered(1))


def _params(*sem):
    return pltpu.CompilerParams(dimension_semantics=sem, vmem_limit_bytes=VMEM_LIMIT)


def _mod_kernel(cc_ref, w_ref, b_ref, o_ref):
    cc = cc_ref[...]
    s = cc * _sigmoid(cc)
    o_ref[0] = jnp.dot(s.astype(BF16), w_ref[0].astype(BF16), preferred_element_type=F32) + b_ref[0]


def _modulation(cc, w_mod, b_mod):
    depth, d, d6 = w_mod.shape
    tn = 1536
    return pl.pallas_call(
        _mod_kernel,
        grid=(depth, d6 // tn),
        in_specs=[
            pl.BlockSpec((SUBLANES, d), lambda l, n: (0, 0)),
            pl.BlockSpec((1, d, tn), lambda l, n: (l, 0, n)),
            pl.BlockSpec((1, 1, tn), lambda l, n: (l, 0, n)),
        ],
        out_specs=pl.BlockSpec((1, SUBLANES, tn), lambda l, n: (l, 0, n)),
        out_shape=jax.ShapeDtypeStruct((depth, SUBLANES, d6), F32),
        compiler_params=_params("parallel", "parallel"),
        name="modulation",
    )(cc, w_mod, b_mod.reshape(depth, 1, d6))


def _inproj_kernel(h_ref, mod_ref, n1g_ref, w_ref, gtab_ref, cos_ref, sin_ref,
                   qd_ref, kdt_ref, qw_ref, kwt_ref, vd_ref, vw_ref, su_ref, gt_ref):
    x = h_ref[0]
    m = mod_ref[0, 0]
    a = (_rms(x, n1g_ref[0]) * (1.0 + m[1:2]) + m[0:1]).astype(BF16)
    tm = x.shape[0]
    lane = lax.broadcasted_iota(jnp.int32, (tm, LANES), 1)
    low_seg = lane < ROPE_DIM
    first_half = (lane % ROPE_DIM) < (ROPE_DIM // 2)
    cosf = cos_ref[...]
    sinf = sin_ref[...]
    for c in range(N_NORM_CHUNKS):
        if c % 2 == 0:
            pair = jnp.dot(a, w_ref[0, :, c * LANES:(c + 2) * LANES], preferred_element_type=F32)
        xc = pair[:, (c % 2) * LANES:(c % 2 + 1) * LANES]
        y = xc * xc
        ss_lo = jnp.sum(jnp.where(low_seg, y, 0.0), axis=-1, keepdims=True)
        ss_hi = jnp.sum(jnp.where(low_seg, 0.0, y), axis=-1, keepdims=True)
        r = jnp.where(low_seg, lax.rsqrt(ss_lo / ROPE_DIM + EPS), lax.rsqrt(ss_hi / ROPE_DIM + EPS))
        xn = xc * r * gtab_ref[0, :, c * LANES:(c + 1) * LANES]
        partner = jnp.where(first_half, pltpu.roll(xn, LANES - ROPE_DIM // 2, 1),
                            pltpu.roll(xn, ROPE_DIM // 2, 1))
        out = xn * cosf + partner * sinf
        if c < 4:
            qd_ref[0, :, c * LANES:(c + 1) * LANES] = out.astype(BF16)
        elif c < 8:
            kdt_ref[0, (c - 4) * LANES:(c - 3) * LANES, :] = out.T.astype(BF16)
        elif c < 12:
            qw_ref[0, :, (c - 8) * LANES:(c - 7) * LANES] = out.astype(BF16)
        else:
            kwt_ref[0] = out.T.astype(BF16)
    vw_ref[0] = pair[:, LANES:2 * LANES].astype(BF16)
    vd_ref[0] = jnp.dot(a, w_ref[0, :, OFF_VD:OFF_SU], preferred_element_type=F32).astype(BF16)
    su_ref[0] = jnp.dot(a, w_ref[0, :, OFF_SU:OFF_GT], preferred_element_type=F32)
    gt_ref[0] = _sigmoid(jnp.dot(a, w_ref[0, :, OFF_GT:], preferred_element_type=F32))


def _inproj(l, h, mods, n1g, w_in_p, gtab, cosf, sinf):
    b, s, d = h.shape
    d_in = w_in_p.shape[-1]
    nt = s // ROW_TILE
    mod_idx = lambda bi, j: (l, jnp.where(j == 0, b, bi), 0, 0)
    tile = lambda width: pl.BlockSpec((1, ROW_TILE, width), lambda bi, j: (bi, j, 0))
    out_shapes = (
        jax.ShapeDtypeStruct((b, s, 512), BF16),
        jax.ShapeDtypeStruct((b, 512, s), BF16),
        jax.ShapeDtypeStruct((b, s, WIN_WIDTH), BF16),
        jax.ShapeDtypeStruct((b, LANES, s), BF16),
        jax.ShapeDtypeStruct((b, s, DIFF_WIDTH), BF16),
        jax.ShapeDtypeStruct((b, s, LANES), BF16),
        jax.ShapeDtypeStruct((b, s, S5_WIDTH), F32),
        jax.ShapeDtypeStruct((b, s, 3 * d), F32),
    )
    out_specs = (
        tile(512),
        pl.BlockSpec((1, 512, ROW_TILE), lambda bi, j: (bi, 0, j)),
        tile(WIN_WIDTH),
        pl.BlockSpec((1, LANES, ROW_TILE), lambda bi, j: (bi, 0, j)),
        tile(DIFF_WIDTH), tile(LANES), tile(S5_WIDTH), tile(3 * d),
    )
    return pl.pallas_call(
        _inproj_kernel,
        grid=(b, nt),
        in_specs=[
            tile(d),
            pl.BlockSpec((1, 1, 6, d), mod_idx),
            _const_spec((1, 1, d), lambda bi, j: (l, 0, 0)),
            _const_spec((1, d, d_in), lambda bi, j: (l, 0, 0)),
            _const_spec((1, 1, N_NORM_CHUNKS * LANES), lambda bi, j: (l, 0, 0)),
            pl.BlockSpec((ROW_TILE, LANES), lambda bi, j: (j, 0)),
            pl.BlockSpec((ROW_TILE, LANES), lambda bi, j: (j, 0)),
        ],
        out_specs=out_specs,
        out_shape=out_shapes,
        compiler_params=_params("parallel", "parallel"),
        name="inproj",
    )(h, mods, n1g, w_in_p, gtab, cosf, sinf)


def _diff_kernel(lam_ref, og_ref, q_ref, kt_ref, v_ref, o_ref, *, lam_init, n_ctx):
    lp = lam_ref[0]
    lam = (jnp.exp(jnp.sum(lp[0:1] * lp[1:2], axis=-1, keepdims=True))
           - jnp.exp(jnp.sum(lp[2:3] * lp[3:4], axis=-1, keepdims=True)) + lam_init)
    og = og_ref[0]
    heads = q_ref.shape[2] // LANES

    def attend(n_keys):
        for hd in range(heads):
            outs = []
            for m in range(2):
                c0 = hd * LANES + m * DIFF_QK_DIM
                qm = q_ref[0, :, c0:c0 + DIFF_QK_DIM]
                km = kt_ref[0, c0:c0 + DIFF_QK_DIM, 0:n_keys]
                s = jnp.dot(qm, km, preferred_element_type=F32)
                e = jnp.exp2(s - jnp.max(s, axis=-1, keepdims=True))
                den = jnp.sum(e, axis=-1, keepdims=True)
                v = v_ref[0, 0:n_keys, hd * DIFF_V_DIM:(hd + 1) * DIFF_V_DIM]
                outs.append(jnp.dot(e.astype(BF16), v, preferred_element_type=F32) / den)
            o = outs[0] - lam * outs[1]
            o_ref[0, :, hd * DIFF_V_DIM:(hd + 1) * DIFF_V_DIM] = (
                _rms(o, og) * (1.0 - lam_init)).astype(o_ref.dtype)

    is_ctx = pl.program_id(2) == 0

    @pl.when(is_ctx)
    def _():
        attend(n_ctx)

    @pl.when(jnp.logical_not(is_ctx))
    def _():
        attend(kt_ref.shape[2])


def _diff_attention(l, lam_init, n_ctx, lam_p, out_g, qd, kdt, vd):
    b, s, _ = qd.shape
    nt = s // ROW_TILE
    hw = DIFF_HEADS_PER_STEP * LANES
    kern = functools.partial(_diff_kernel, lam_init=lam_init, n_ctx=n_ctx)
    return pl.pallas_call(
        kern,
        grid=(b, DIFF_HEADS // DIFF_HEADS_PER_STEP, nt),
        in_specs=[
            _const_spec((1, 4, DIFF_QK_DIM), lambda bi, hd, j: (l, 0, 0)),
            _const_spec((1, 1, DIFF_V_DIM), lambda bi, hd, j: (l, 0, 0)),
            pl.BlockSpec((1, ROW_TILE, hw), lambda bi, hd, j: (bi, j, hd)),
            pl.BlockSpec((1, hw, s), lambda bi, hd, j: (bi, hd, 0)),
            pl.BlockSpec((1, s, hw), lambda bi, hd, j: (bi, 0, hd)),
        ],
        out_specs=pl.BlockSpec((1, ROW_TILE, hw), lambda bi, hd, j: (bi, j, hd)),
        out_shape=jax.ShapeDtypeStruct((b, s, DIFF_WIDTH), BF16),
        compiler_params=_params("parallel", "parallel", "parallel"),
        name="diff_attention",
    )(lam_p, out_g, qd, kdt, vd)


def _win_kernel(sink_ref, q_ref, kt_ref, v_ref, o_ref, *, layer, n_ctx):
    j = pl.program_id(1)
    s_total = kt_ref.shape[2]
    band = 3 * QBLOCK
    q = q_ref[0]
    rows = WIN_GROUP * QBLOCK
    row_group = lax.broadcasted_iota(jnp.int32, (rows, 1), 0) // QBLOCK

    def run(with_band):
        if with_band:
            start = pl.multiple_of(QBLOCK * jnp.minimum(j - 1, (s_total - band) // QBLOCK), QBLOCK)
            qpos = QBLOCK * j + lax.broadcasted_iota(jnp.int32, (QBLOCK, band), 0)
            kpos = start + lax.broadcasted_iota(jnp.int32, (QBLOCK, band), 1)
            valid = jnp.where(kpos >= n_ctx, jnp.abs(kpos - qpos), WINDOW + 1) <= WINDOW
            bias = jnp.concatenate([jnp.where(valid, 0.0, NEG_INF)] * WIN_GROUP, axis=0)
        for kv in range(WIN_KV_HEADS):
            q4 = jnp.concatenate(
                [q[:, (kv * WIN_GROUP + g) * WIN_HEAD_DIM:(kv * WIN_GROUP + g + 1) * WIN_HEAD_DIM]
                 for g in range(WIN_GROUP)], axis=0)
            sink = jnp.zeros((rows, 1), F32)
            for g in range(WIN_GROUP):
                sink = jnp.where(row_group == g, sink_ref[layer, kv * WIN_GROUP + g] * LOG2E, sink)
            ksl = slice(kv * WIN_HEAD_DIM, (kv + 1) * WIN_HEAD_DIM)
            s_c = jnp.dot(q4, kt_ref[0, ksl, 0:n_ctx], preferred_element_type=F32)
            mx = jnp.maximum(jnp.max(s_c, axis=-1, keepdims=True), sink)
            if with_band:
                s_b = jnp.dot(q4, kt_ref[0, ksl, pl.ds(start, band)], preferred_element_type=F32) + bias
                mx = jnp.maximum(mx, jnp.max(s_b, axis=-1, keepdims=True))
            e_c = jnp.exp2(s_c - mx)
            den = jnp.sum(e_c, axis=-1, keepdims=True) + jnp.exp2(sink - mx)
            o = jnp.dot(e_c.astype(BF16), v_ref[0, 0:n_ctx, ksl], preferred_element_type=F32)
            if with_band:
                e_b = jnp.exp2(s_b - mx)
                den = den + jnp.sum(e_b, axis=-1, keepdims=True)
                o = o + jnp.dot(e_b.astype(BF16), v_ref[0, pl.ds(start, band), ksl],
                                preferred_element_type=F32)
            o = o / den
            for g in range(WIN_GROUP):
                c0 = (kv * WIN_GROUP + g) * WIN_HEAD_DIM
                o_ref[0, :, c0:c0 + WIN_HEAD_DIM] = o[g * QBLOCK:(g + 1) * QBLOCK].astype(o_ref.dtype)

    is_ctx = j < n_ctx // QBLOCK

    @pl.when(is_ctx)
    def _():
        run(False)

    @pl.when(jnp.logical_not(is_ctx))
    def _():
        run(True)


def _window_attention(l, n_ctx, sink, qw, kwt, vw):
    b, s, _ = qw.shape
    kern = functools.partial(_win_kernel, layer=l, n_ctx=n_ctx)
    return pl.pallas_call(
        kern,
        grid=(b, s // QBLOCK),
        in_specs=[
            pl.BlockSpec(memory_space=pltpu.SMEM),
            pl.BlockSpec((1, QBLOCK, WIN_WIDTH), lambda bi, j: (bi, j, 0)),
            pl.BlockSpec((1, LANES, s), lambda bi, j: (bi, 0, 0)),
            pl.BlockSpec((1, s, LANES), lambda bi, j: (bi, 0, 0)),
        ],
        out_specs=pl.BlockSpec((1, QBLOCK, WIN_WIDTH), lambda bi, j: (bi, j, 0)),
        out_shape=jax.ShapeDtypeStruct((b, s, WIN_WIDTH), BF16),
        compiler_params=_params("parallel", "parallel"),
        name="window_attention",
    )(sink, qw, kwt, vw)


def _s5_kernel(uf_ref, ub_ref, bb_ref, cc_ref, are_ref, aim_ref, yf_ref, yb_ref,
               st_ref, slab_ref, buf_ref, of_ref, ob_ref):
    @pl.when(pl.program_id(0) == 0)
    def _():
        st_ref[...] = jnp.zeros_like(st_ref)

    nb, steps, _ = uf_ref.shape
    rows = steps * SUBLANES
    ns = S5_BLOCK_STATES
    fwd_sub = lax.broadcasted_iota(jnp.int32, (SUBLANES, LANES), 0) < nb
    fwd = (lax.broadcasted_iota(jnp.int32, (rows, 1), 0) % SUBLANES) < nb
    for jb in range(S5_BLOCKS):
        lanes = slice(jb * LANES, (jb + 1) * LANES)
        for k in range(nb):
            slab_ref[jb, k * S5_PITCH:k * S5_PITCH + steps, :] = uf_ref[k, :, lanes]
            slab_ref[jb, (nb + k) * S5_PITCH:(nb + k) * S5_PITCH + steps, :] = ub_ref[k, :, lanes]
        lhs_f, lhs_b = [], []
        for i in range(steps):
            a = slab_ref[jb, pl.ds(i, SUBLANES, stride=S5_PITCH), :]
            b = slab_ref[jb, pl.ds(steps - 1 - i, SUBLANES, stride=S5_PITCH), :]
            lhs_f.append(jnp.where(fwd_sub, a, 0.0))
            lhs_b.append(jnp.where(fwd_sub, 0.0, b))
        lhs = jnp.concatenate([jnp.concatenate(lhs_f, axis=0).astype(BF16),
                               jnp.concatenate(lhs_b, axis=0).astype(BF16)], axis=1)
        buf = buf_ref.at[jb % 2]
        buf[...] = jnp.dot(lhs, bb_ref[0, jb], preferred_element_type=F32)
        a_re = are_ref[0, :, jb * ns:(jb + 1) * ns]
        a_im = aim_ref[0, :, jb * ns:(jb + 1) * ns]
        s_re = st_ref[0, :, jb * ns:(jb + 1) * ns]
        s_im = st_ref[1, :, jb * ns:(jb + 1) * ns]
        for t in range(steps):
            r = slice(t * SUBLANES, (t + 1) * SUBLANES)
            n_re = a_re * s_re - a_im * s_im + buf[r, 0:ns]
            n_im = a_re * s_im + a_im * s_re + buf[r, ns:2 * ns]
            buf[r, 0:ns] = n_re
            buf[r, ns:2 * ns] = n_im
            s_re, s_im = n_re, n_im
        st_ref[0, :, jb * ns:(jb + 1) * ns] = s_re
        st_ref[1, :, jb * ns:(jb + 1) * ns] = s_im
        y = jnp.dot(buf[...].astype(BF16), cc_ref[0, jb], preferred_element_type=F32)
        y = jnp.where(fwd, y[:, 0:LANES], y[:, LANES:2 * LANES])
        for i in range(steps):
            v = y[i * SUBLANES:(i + 1) * SUBLANES]
            of_ref[jb, pl.ds(i, SUBLANES, stride=S5_PITCH), :] = v
            ob_ref[jb, pl.ds(steps - 1 - i, SUBLANES, stride=S5_PITCH), :] = v
        for k in range(nb):
            yf_ref[k, :, lanes] = of_ref[jb, k * S5_PITCH:k * S5_PITCH + steps, :]
            yb_ref[k, :, lanes] = ob_ref[jb, (nb + k) * S5_PITCH:(nb + k) * S5_PITCH + steps, :]


def _s5_scan(l, n_ctx, su, bbcat, ccat, a_re, a_im):
    b, s, width = su.shape
    rows = S5_CHUNK * SUBLANES
    n_states = S5_GROUPS * S5_STATE
    n_tiles = s // S5_CHUNK
    ctx_tiles = n_ctx // S5_CHUNK
    mirror = lambda c: jnp.where(c < ctx_tiles, ctx_tiles - 1 - c, n_tiles + ctx_tiles - 1 - c)
    fwd_spec = pl.BlockSpec((b, S5_CHUNK, width), lambda c: (0, c, 0))
    bwd_spec = pl.BlockSpec((b, S5_CHUNK, width), lambda c: (0, mirror(c), 0))
    slab = pltpu.VMEM((width // LANES, SUBLANES * S5_PITCH, LANES), F32)
    return pl.pallas_call(
        _s5_kernel,
        grid=(n_tiles,),
        in_specs=[
            fwd_spec, bwd_spec,
            _const_spec((1, S5_BLOCKS, 2 * LANES, 2 * S5_BLOCK_STATES), lambda c: (l, 0, 0, 0)),
            _const_spec((1, S5_BLOCKS, 2 * S5_BLOCK_STATES, 2 * LANES), lambda c: (l, 0, 0, 0)),
            _const_spec((1, SUBLANES, n_states), lambda c: (l, 0, 0)),
            _const_spec((1, SUBLANES, n_states), lambda c: (l, 0, 0)),
        ],
        out_specs=(fwd_spec, bwd_spec),
        out_shape=(jax.ShapeDtypeStruct((b, s, width), F32), jax.ShapeDtypeStruct((b, s, width), F32)),
        scratch_shapes=[
            pltpu.VMEM((2, SUBLANES, n_states), F32),
            slab,
            pltpu.VMEM((2, rows, 2 * S5_BLOCK_STATES), F32),
            slab, slab,
        ],
        compiler_params=_params("arbitrary"),
        name="s5_scan",
    )(su, su, bbcat, ccat, a_re, a_im)


def _gelu_tanh(x):
    return 0.5 * x * (1.0 + jnp.tanh(math.sqrt(2.0 / math.pi) * (x + 0.044715 * (x * x * x))))


def _merge_kernel(h_ref, mod_ref, yd_ref, su_ref, yf_ref, yb_ref, yw_ref, gt_ref,
                  dsk_ref, wglu_ref, wd_ref, ws_ref, ww_ref, wo_ref, o_ref):
    d = h_ref.shape[2]
    y = su_ref[0] * dsk_ref[0] + yf_ref[0] + yb_ref[0]
    g = _gelu_tanh(y)
    ys = g * _sigmoid(jnp.dot(g.astype(BF16), wglu_ref[0], preferred_element_type=F32))
    m = (gt_ref[0, :, 0:d] * jnp.dot(yd_ref[0], wd_ref[0], preferred_element_type=F32)
         + gt_ref[0, :, d:2 * d] * jnp.dot(ys.astype(BF16), ws_ref[0], preferred_element_type=F32)
         + gt_ref[0, :, 2 * d:3 * d] * jnp.dot(yw_ref[0], ww_ref[0], preferred_element_type=F32))
    g1 = mod_ref[0, 0, 2:3]
    o_ref[0] = h_ref[0] + g1 * jnp.dot(m.astype(BF16), wo_ref[0], preferred_element_type=F32)


def _merge(l, h, mods, yd, su, yf, yb, yw, gates, dsk, wglu, wd, ws, ww, wo):
    b, s, d = h.shape
    nt = s // ROW_TILE
    mod_idx = lambda bi, j: (l, jnp.where(j == 0, b, bi), 0, 0)
    tile = lambda width: pl.BlockSpec((1, ROW_TILE, width), lambda bi, j: (bi, j, 0))
    wspec = lambda r, c: _const_spec((1, r, c), lambda bi, j: (l, 0, 0))
    return pl.pallas_call(
        _merge_kernel,
        grid=(b, nt),
        in_specs=[
            tile(d), pl.BlockSpec((1, 1, 6, d), mod_idx),
            tile(DIFF_WIDTH), tile(S5_WIDTH), tile(S5_WIDTH), tile(S5_WIDTH), tile(WIN_WIDTH), tile(3 * d),
            wspec(1, S5_WIDTH), wspec(S5_WIDTH, S5_WIDTH),
            wspec(DIFF_WIDTH, d), wspec(S5_WIDTH, d), wspec(WIN_WIDTH, d), wspec(d, d),
        ],
        out_specs=tile(d),
        out_shape=jax.ShapeDtypeStruct((b, s, d), F32),
        compiler_params=_params("parallel", "parallel"),
        name="merge",
    )(h, mods, yd, su, yf, yb, yw, gates, dsk, wglu, wd, ws, ww, wo)


def _ffn_kernel(h_ref, mod_ref, n2g_ref, w1_ref, w2_ref, o_ref, *, latent_only):
    def run():
        x = h_ref[0]
        m = mod_ref[0, 0]
        f = (_rms(x, n2g_ref[0]) * (1.0 + m[4:5]) + m[3:4]).astype(BF16)
        t = jnp.maximum(jnp.dot(f, w1_ref[0], preferred_element_type=F32), 0.0)
        t = (t * t).astype(BF16)
        o_ref[0] = x + m[5:6] * jnp.dot(t, w2_ref[0], preferred_element_type=F32)

    if latent_only:
        pl.when(pl.program_id(1) > 0)(run)
    else:
        run()


def _ffn(l, h, mods, n2g, w1, w2, latent_only):
    b, s, d = h.shape
    nt = s // ROW_TILE
    d_ff = w1.shape[-1]
    mod_idx = lambda bi, j: (l, jnp.where(j == 0, b, bi), 0, 0)
    tile = pl.BlockSpec((1, ROW_TILE, d), lambda bi, j: (bi, j, 0))
    if latent_only:
        out_spec = pl.BlockSpec((1, ROW_TILE, d), lambda bi, j: (bi, jnp.maximum(j - 1, 0), 0))
        out_rows = s - ROW_TILE
    else:
        out_spec, out_rows = tile, s
    return pl.pallas_call(
        functools.partial(_ffn_kernel, latent_only=latent_only),
        grid=(b, nt),
        in_specs=[
            tile, pl.BlockSpec((1, 1, 6, d), mod_idx),
            _const_spec((1, 1, d), lambda bi, j: (l, 0, 0)),
            _const_spec((1, d, d_ff), lambda bi, j: (l, 0, 0)),
            _const_spec((1, d_ff, d), lambda bi, j: (l, 0, 0)),
        ],
        out_specs=out_spec,
        out_shape=jax.ShapeDtypeStruct((b, out_rows, d), F32),
        compiler_params=_params("parallel", "arbitrary"),
        name="ffn",
    )(h, mods, n2g, w1, w2)


def _permute_inproj(w_in):
    depth, d, _ = w_in.shape
    qk = w_in[:, :, 0:1024].reshape(depth, d, 2, 2, DIFF_HEADS, DIFF_QK_DIM)
    qk = qk.transpose(0, 1, 2, 4, 3, 5).reshape(depth, d, 1024)
    parts = [qk, w_in[:, :, 2048:2816], w_in[:, :, 1024:1536], w_in[:, :, 1536:2048], w_in[:, :, 2816:]]
    return jnp.concatenate(parts, -1).astype(BF16)


def _rope_tables(n_ctx, n_lat):
    n_freq = ROPE_DIM // 4
    inv = ROPE_BASE ** (-jnp.arange(n_freq, dtype=F32) / n_freq)
    rows = n_lat // GRID_W
    r = jnp.repeat(jnp.arange(rows, dtype=F32), GRID_W)
    col = jnp.tile(jnp.arange(GRID_W, dtype=F32), rows)
    ang = jnp.concatenate([r[:, None] * inv, col[:, None] * inv], -1)
    cos, sin = jnp.cos(ang), jnp.sin(ang)
    cosf = jnp.concatenate([jnp.ones((n_ctx, LANES), F32), jnp.tile(cos, (1, 4))], 0)
    sinf = jnp.concatenate([jnp.zeros((n_ctx, LANES), F32),
                            jnp.tile(jnp.concatenate([-sin, sin], -1), (1, 2))], 0)
    return cosf, sinf


def _s5_disc_kernel(lre_ref, lim_ref, ldt_ref, bre_ref, bim_ref, are_ref, aim_ref, bbre_ref, bbim_ref):
    lre = lre_ref[...]
    lim = lim_ref[...]
    dt = jnp.exp(ldt_ref[...])
    mag = jnp.exp(lre * dt)
    ang = lim * dt
    a_re = mag * jnp.cos(ang)
    a_im = mag * jnp.sin(ang)
    den = lre * lre + lim * lim
    n_re = a_re - 1.0
    f_re = (n_re * lre + a_im * lim) / den
    f_im = (a_im * lre - n_re * lim) / den
    are_ref[...] = a_re
    aim_ref[...] = a_im
    b_re = bre_ref[...]
    b_im = bim_ref[...]
    bbre_ref[...] = f_re * b_re - f_im * b_im
    bbim_ref[...] = f_re * b_im + f_im * b_re


def _s5_discretize(lam_re, lam_im, log_dt, b_re, b_im):
    depth, two, g, p = lam_re.shape
    hg = b_re.shape[-1]
    r = depth * two * g
    row = lambda a: a.reshape(r, 1, p)
    bt = lambda a: a.reshape(r, p, hg).transpose(0, 2, 1)
    vec = jax.ShapeDtypeStruct((r, 1, p), F32)
    mat = jax.ShapeDtypeStruct((r, hg, p), F32)
    a_re, a_im, bb_re, bb_im = pl.pallas_call(
        _s5_disc_kernel, out_shape=(vec, vec, mat, mat), name="s5_discretize",
    )(row(lam_re), row(lam_im), log_dt.reshape(r, 1, 1), bt(b_re), bt(b_im))
    shape = (depth, two, g, hg, p)
    return (a_re.reshape(depth, two, g, p), a_im.reshape(depth, two, g, p),
            bb_re.reshape(shape), bb_im.reshape(shape))


def _s5_tables(lam_re, lam_im, log_dt, b_re, b_im, c_re, c_im):
    depth = lam_re.shape[0]
    a_re, a_im, bb_re, bb_im = _s5_discretize(lam_re, lam_im, log_dt, b_re, b_im)
    bb_re = bb_re.transpose(0, 1, 2, 4, 3)
    bb_im = bb_im.transpose(0, 1, 2, 4, 3)
    eye = jnp.eye(S5_BLOCK_GROUPS, dtype=F32)
    shp = (depth, 2, S5_BLOCKS, S5_BLOCK_GROUPS, S5_STATE, S5_GROUP)

    def in_block(bb):
        t = jnp.einsum('ldjgph,gk->ldjghkp', bb.reshape(shp), eye)
        return t.reshape(depth, 2, S5_BLOCKS, LANES, S5_BLOCK_STATES).transpose(0, 2, 1, 3, 4)

    bbcat = jnp.concatenate([in_block(bb_re), in_block(bb_im)], -1)
    bbcat = bbcat.reshape(depth, S5_BLOCKS, 2 * LANES, 2 * S5_BLOCK_STATES).astype(BF16)
    shc = (depth, 2, S5_BLOCKS, S5_BLOCK_GROUPS, S5_GROUP, S5_STATE)

    def out_block(c):
        t = jnp.einsum('ldjghp,gk->ljgpdkh', c.reshape(shc), eye)
        return t.reshape(depth, S5_BLOCKS, S5_BLOCK_STATES, 2 * LANES)

    ccat = jnp.concatenate([out_block(c_re), out_block(-c_im)], 2).astype(BF16)
    n_states = S5_GROUPS * S5_STATE
    rep = lambda a: jnp.repeat(a.reshape(depth, 2, n_states), SUBLANES // 2, axis=1)
    return bbcat, ccat, rep(a_re), rep(a_im)


def kernel(x, c, ctx, c_ctx, w_mod, b_mod, norm1_g, norm2_g, w_in, diff_q_norm_g, diff_k_norm_g, diff_lam_q1, diff_lam_k1, diff_lam_q2, diff_lam_k2, diff_out_norm_g, s5_lambda_re, s5_lambda_im, s5_log_dt, s5_b_re, s5_b_im, s5_c_re, s5_c_im, s5_d, s5_w_glu, win_q_norm_g, win_k_norm_g, win_sink, w_proj_diff, w_proj_s5, w_proj_win, w_out, w_ff1, w_ff2):
    b, n_lat, d = x.shape
    n_ctx = ctx.shape[1]
    s = n_ctx + n_lat
    depth = w_in.shape[0]
    assert n_ctx == ROW_TILE and n_lat % ROW_TILE == 0 and b == SUBLANES // 2

    cc = jnp.zeros((SUBLANES, d), F32).at[:b].set(c).at[b].set(c_ctx)
    mods = _modulation(cc, w_mod, b_mod).reshape(depth, SUBLANES, 6, d)

    w_in_p = _permute_inproj(w_in)
    scale_d = DIFF_QK_DIM ** -0.5 * LOG2E
    scale_w = WIN_HEAD_DIM ** -0.5 * LOG2E
    gtab = jnp.concatenate([
        jnp.tile(diff_q_norm_g * scale_d, (1, 8)), jnp.tile(diff_k_norm_g, (1, 8)),
        jnp.tile(win_q_norm_g * scale_w, (1, 8)), jnp.tile(win_k_norm_g, (1, 2))], -1)[:, None, :]
    cosf, sinf = _rope_tables(n_ctx, n_lat)
    lam_p = jnp.stack([diff_lam_q1, diff_lam_k1, diff_lam_q2, diff_lam_k2], 1)
    bbcat, ccat, a_re, a_im = _s5_tables(s5_lambda_re, s5_lambda_im, s5_log_dt, s5_b_re, s5_b_im,
                                         s5_c_re, s5_c_im)
    bf = lambda w: w.astype(BF16)
    wglu, wd, ws, ww, wo, w1, w2 = (bf(s5_w_glu), bf(w_proj_diff), bf(w_proj_s5), bf(w_proj_win),
                                    bf(w_out), bf(w_ff1), bf(w_ff2))
    n1g, n2g = norm1_g[:, None, :], norm2_g[:, None, :]
    og, dsk = diff_out_norm_g[:, None, :], s5_d[:, None, :]

    h = jnp.concatenate([ctx, x], axis=1)
    for l in range(depth):
        lam_init = 0.8 - 0.6 * math.exp(-0.3 * l)
        qd, kdt, qw, kwt, vd, vw, su, gates = _inproj(l, h, mods, n1g, w_in_p, gtab, cosf, sinf)
        yd = _diff_attention(l, lam_init, n_ctx, lam_p, og, qd, kdt, vd)
        yw = _window_attention(l, n_ctx, win_sink, qw, kwt, vw)
        yf, yb = _s5_scan(l, n_ctx, su, bbcat, ccat, a_re, a_im)
        h = _merge(l, h, mods, yd, su, yf, yb, yw, gates, dsk, wglu, wd, ws, ww, wo)
        h = _ffn(l, h, mods, n2g, w1, w2, latent_only=(l == depth - 1))
    return h
```

```python
import functools
import math

import numpy as np
import jax
import jax.numpy as jnp
from jax import lax
from jax.experimental import pallas as pl
from jax.experimental.pallas import tpu as pltpu

F32 = jnp.float32
BF16 = jnp.bfloat16

GRID_W = 64
QBLOCK = 128
DIFF_HEADS = 4
DIFF_QK_DIM = 64
DIFF_V_DIM = 128
DIFF_WIDTH = DIFF_HEADS * DIFF_V_DIM
S5_WIDTH = 512
S5_GROUP = 16
S5_GROUPS = 32
S5_STATE = 64
WIN_Q_HEADS = 8
WIN_KV_HEADS = 2
WIN_HEAD_DIM = 64
WIN_GROUP = WIN_Q_HEADS // WIN_KV_HEADS
WIN_WIDTH = WIN_Q_HEADS * WIN_HEAD_DIM
WINDOW = 128
ROPE_DIM = 64
ROPE_BASE = 10000.0
EPS = 1e-6
NEG_INF = -1e30

LANES = 128
SUBLANES = 8
ROW_TILE = 256
DIFF_HEADS_PER_STEP = 4
S5_CHUNK = 128
S5_BLOCK_GROUPS = LANES // S5_GROUP
S5_BLOCKS = S5_GROUPS // S5_BLOCK_GROUPS
S5_BLOCK_STATES = S5_BLOCK_GROUPS * S5_STATE
S5_PITCH = S5_CHUNK + SUBLANES
LOG2E = math.log2(math.e)
VMEM_LIMIT = 48 * 1024 * 1024

N_NORM_CHUNKS = 13
OFF_QD, OFF_KD, OFF_QW, OFF_KW = 0, 512, 1024, 1536
OFF_VW, OFF_VD, OFF_SU, OFF_GT = 1664, 1792, 2304, 2816


def _sigmoid(x):
    return 1.0 / (1.0 + jnp.exp(-x))


def _rms(x, g):
    return x * lax.rsqrt(jnp.mean(x * x, axis=-1, keepdims=True) + EPS) * g


def _const_spec(shape, index_map):
    return pl.BlockSpec(shape, index_map, pipeline_mode=pl.Buffered(1))


def _params(*sem):
    return pltpu.CompilerParams(dimension_semantics=sem, vmem_limit_bytes=VMEM_LIMIT)


def _mod_kernel(cc_ref, w_ref, b_ref, o_ref):
    cc = cc_ref[...]
    s = cc * _sigmoid(cc)
    o_ref[0] = jnp.dot(s.astype(BF16), w_ref[0].astype(BF16), preferred_element_type=F32) + b_ref[0]


def _modulation(cc, w_mod, b_mod):
    depth, d, d6 = w_mod.shape
    tn = 1536
    return pl.pallas_call(
        _mod_kernel,
        grid=(depth, d6 // tn),
        in_specs=[
            pl.BlockSpec((SUBLANES, d), lambda l, n: (0, 0)),
            pl.BlockSpec((1, d, tn), lambda l, n: (l, 0, n)),
            pl.BlockSpec((1, 1, tn), lambda l, n: (l, 0, n)),
        ],
        out_specs=pl.BlockSpec((1, SUBLANES, tn), lambda l, n: (l, 0, n)),
        out_shape=jax.ShapeDtypeStruct((depth, SUBLANES, d6), F32),
        compiler_params=_params("parallel", "parallel"),
        name="modulation",
    )(cc, w_mod, b_mod.reshape(depth, 1, d6))


def _inproj_kernel(h_ref, mod_ref, n1g_ref, w_ref, gtab_ref, cos_ref, sin_ref,
                   qd_ref, kdt_ref, qw_ref, kwt_ref, vd_ref, vw_ref, su_ref, gt_ref):
    x = h_ref[0]
    m = mod_ref[0, 0]
    a = (_rms(x, n1g_ref[0]) * (1.0 + m[1:2]) + m[0:1]).astype(BF16)
    tm = x.shape[0]
    lane = lax.broadcasted_iota(jnp.int32, (tm, LANES), 1)
    low_seg = lane < ROPE_DIM
    first_half = (lane % ROPE_DIM) < (ROPE_DIM // 2)
    cosf = cos_ref[...]
    sinf = sin_ref[...]
    for c in range(N_NORM_CHUNKS):
        if c % 2 == 0:
            pair = jnp.dot(a, w_ref[0, :, c * LANES:(c + 2) * LANES], preferred_element_type=F32)
        xc = pair[:, (c % 2) * LANES:(c % 2 + 1) * LANES]
        y = xc * xc
        ss_lo = jnp.sum(jnp.where(low_seg, y, 0.0), axis=-1, keepdims=True)
        ss_hi = jnp.sum(jnp.where(low_seg, 0.0, y), axis=-1, keepdims=True)
        r = jnp.where(low_seg, lax.rsqrt(ss_lo / ROPE_DIM + EPS), lax.rsqrt(ss_hi / ROPE_DIM + EPS))
        xn = xc * r * gtab_ref[0, :, c * LANES:(c + 1) * LANES]
        partner = jnp.where(first_half, pltpu.roll(xn, LANES - ROPE_DIM // 2, 1),
                            pltpu.roll(xn, ROPE_DIM // 2, 1))
        out = xn * cosf + partner * sinf
        if c < 4:
            qd_ref[0, :, c * LANES:(c + 1) * LANES] = out.astype(BF16)
        elif c < 8:
            kdt_ref[0, (c - 4) * LANES:(c - 3) * LANES, :] = out.T.astype(BF16)
        elif c < 12:
            qw_ref[0, :, (c - 8) * LANES:(c - 7) * LANES] = out.astype(BF16)
        else:
            kwt_ref[0] = out.T.astype(BF16)
    vw_ref[0] = pair[:, LANES:2 * LANES].astype(BF16)
    vd_ref[0] = jnp.dot(a, w_ref[0, :, OFF_VD:OFF_SU], preferred_element_type=F32).astype(BF16)
    su_ref[0] = jnp.dot(a, w_ref[0, :, OFF_SU:OFF_GT], preferred_element_type=F32)
    gt_ref[0] = _sigmoid(jnp.dot(a, w_ref[0, :, OFF_GT:], preferred_element_type=F32))


def _inproj(l, h, mods, n1g, w_in_p, gtab, cosf, sinf):
    b, s, d = h.shape
    d_in = w_in_p.shape[-1]
    nt = s // ROW_TILE
    mod_idx = lambda bi, j: (l, jnp.where(j == 0, b, bi), 0, 0)
    tile = lambda width: pl.BlockSpec((1, ROW_TILE, width), lambda bi, j: (bi, j, 0))
    out_shapes = (
        jax.ShapeDtypeStruct((b, s, 512), BF16),
        jax.ShapeDtypeStruct((b, 512, s), BF16),
        jax.ShapeDtypeStruct((b, s, WIN_WIDTH), BF16),
        jax.ShapeDtypeStruct((b, LANES, s), BF16),
        jax.ShapeDtypeStruct((b, s, DIFF_WIDTH), BF16),
        jax.ShapeDtypeStruct((b, s, LANES), BF16),
        jax.ShapeDtypeStruct((b, s, S5_WIDTH), F32),
        jax.ShapeDtypeStruct((b, s, 3 * d), F32),
    )
    tile_t = lambda width: pl.BlockSpec((1, width, ROW_TILE), lambda bi, j: (bi, 0, j))
    out_specs = (
        tile(512), tile_t(512), tile(WIN_WIDTH), tile_t(LANES),
        tile(DIFF_WIDTH), tile(LANES), tile(S5_WIDTH), tile(3 * d),
    )
    return pl.pallas_call(
        _inproj_kernel,
        grid=(b, nt),
        in_specs=[
            tile(d),
            pl.BlockSpec((1, 1, 6, d), mod_idx),
            _const_spec((1, 1, d), lambda bi, j: (l, 0, 0)),
            _const_spec((1, d, d_in), lambda bi, j: (l, 0, 0)),
            _const_spec((1, 1, N_NORM_CHUNKS * LANES), lambda bi, j: (l, 0, 0)),
            pl.BlockSpec((ROW_TILE, LANES), lambda bi, j: (j, 0)),
            pl.BlockSpec((ROW_TILE, LANES), lambda bi, j: (j, 0)),
        ],
        out_specs=out_specs,
        out_shape=out_shapes,
        compiler_params=_params("parallel", "parallel"),
        name="inproj",
    )(h, mods, n1g, w_in_p, gtab, cosf, sinf)


def _diff_kernel(lam_ref, og_ref, q_ref, kt_ref, v_ref, o_ref, *, lam_init, n_ctx):
    lp = lam_ref[0]
    lam = (jnp.exp(jnp.sum(lp[0:1] * lp[1:2], axis=-1, keepdims=True))
           - jnp.exp(jnp.sum(lp[2:3] * lp[3:4], axis=-1, keepdims=True)) + lam_init)
    og = og_ref[0]
    heads = q_ref.shape[2] // LANES

    def attend(n_keys):
        for hd in range(heads):
            v = v_ref[0, 0:n_keys, hd * DIFF_V_DIM:(hd + 1) * DIFF_V_DIM]
            v1 = jnp.concatenate([v, jnp.ones_like(v)], axis=1)
            outs = []
            for m in range(2):
                c0 = hd * LANES + m * DIFF_QK_DIM
                qm = q_ref[0, :, c0:c0 + DIFF_QK_DIM]
                km = kt_ref[0, c0:c0 + DIFF_QK_DIM, 0:n_keys]
                s = jnp.dot(qm, km, preferred_element_type=F32)
                e = jnp.exp2(s - jnp.max(s, axis=-1, keepdims=True))
                pv = jnp.dot(e.astype(BF16), v1, preferred_element_type=F32)
                outs.append(pv[:, 0:DIFF_V_DIM] / pv[:, DIFF_V_DIM:2 * DIFF_V_DIM])
            o = outs[0] - lam * outs[1]
            o_ref[0, :, hd * DIFF_V_DIM:(hd + 1) * DIFF_V_DIM] = (
                _rms(o, og) * (1.0 - lam_init)).astype(o_ref.dtype)

    is_ctx = pl.program_id(2) == 0

    @pl.when(is_ctx)
    def _():
        attend(n_ctx)

    @pl.when(jnp.logical_not(is_ctx))
    def _():
        attend(kt_ref.shape[2])


def _diff_attention(l, lam_init, n_ctx, lam_p, out_g, qd, kdt, vd):
    b, s, _ = qd.shape
    nt = s // ROW_TILE
    hw = DIFF_HEADS_PER_STEP * LANES
    kern = functools.partial(_diff_kernel, lam_init=lam_init, n_ctx=n_ctx)
    return pl.pallas_call(
        kern,
        grid=(b, DIFF_HEADS // DIFF_HEADS_PER_STEP, nt),
        in_specs=[
            _const_spec((1, 4, DIFF_QK_DIM), lambda bi, hd, j: (l, 0, 0)),
            _const_spec((1, 1, DIFF_V_DIM), lambda bi, hd, j: (l, 0, 0)),
            pl.BlockSpec((1, ROW_TILE, hw), lambda bi, hd, j: (bi, j, hd)),
            pl.BlockSpec((1, hw, s), lambda bi, hd, j: (bi, hd, 0)),
            pl.BlockSpec((1, s, hw), lambda bi, hd, j: (bi, 0, hd)),
        ],
        out_specs=pl.BlockSpec((1, ROW_TILE, hw), lambda bi, hd, j: (bi, j, hd)),
        out_shape=jax.ShapeDtypeStruct((b, s, DIFF_WIDTH), BF16),
        compiler_params=_params("parallel", "parallel", "parallel"),
        name="diff_attention",
    )(lam_p, out_g, qd, kdt, vd)


def _win_kernel(sink_ref, q_ref, kt_ref, v_ref, o_ref, *, layer, n_ctx):
    j = pl.program_id(1)
    s_total = kt_ref.shape[2]
    band = 3 * QBLOCK
    q = q_ref[0]
    rows = WIN_GROUP * QBLOCK
    row_group = lax.broadcasted_iota(jnp.int32, (rows, 1), 0) // QBLOCK

    def run(with_band):
        if with_band:
            start = pl.multiple_of(QBLOCK * jnp.minimum(j - 1, (s_total - band) // QBLOCK), QBLOCK)
            qpos = QBLOCK * j + lax.broadcasted_iota(jnp.int32, (QBLOCK, band), 0)
            kpos = start + lax.broadcasted_iota(jnp.int32, (QBLOCK, band), 1)
            valid = jnp.where(kpos >= n_ctx, jnp.abs(kpos - qpos), WINDOW + 1) <= WINDOW
            bias = jnp.concatenate([jnp.where(valid, 0.0, NEG_INF)] * WIN_GROUP, axis=0)
            v_b = v_ref[0, pl.ds(start, band), :]
            vb1 = jnp.concatenate([v_b, jnp.ones_like(v_b)], axis=1)
        v_c = v_ref[0, 0:n_ctx, :]
        vc1 = jnp.concatenate([v_c, jnp.ones_like(v_c)], axis=1)
        for kv in range(WIN_KV_HEADS):
            q4 = jnp.concatenate(
                [q[:, (kv * WIN_GROUP + g) * WIN_HEAD_DIM:(kv * WIN_GROUP + g + 1) * WIN_HEAD_DIM]
                 for g in range(WIN_GROUP)], axis=0)
            sink = jnp.zeros((rows, 1), F32)
            for g in range(WIN_GROUP):
                sink = jnp.where(row_group == g, sink_ref[layer, kv * WIN_GROUP + g] * LOG2E, sink)
            ksl = slice(kv * WIN_HEAD_DIM, (kv + 1) * WIN_HEAD_DIM)
            s_c = jnp.dot(q4, kt_ref[0, ksl, 0:n_ctx], preferred_element_type=F32)
            mx = jnp.maximum(jnp.max(s_c, axis=-1, keepdims=True), sink)
            if with_band:
                s_b = jnp.dot(q4, kt_ref[0, ksl, pl.ds(start, band)], preferred_element_type=F32) + bias
                mx = jnp.maximum(mx, jnp.max(s_b, axis=-1, keepdims=True))
            pv = jnp.dot(jnp.exp2(s_c - mx).astype(BF16), vc1, preferred_element_type=F32)
            if with_band:
                pv = pv + jnp.dot(jnp.exp2(s_b - mx).astype(BF16), vb1, preferred_element_type=F32)
            den = pv[:, LANES:LANES + WIN_HEAD_DIM] + jnp.exp2(sink - mx)
            o = pv[:, ksl] / den
            for g in range(WIN_GROUP):
                c0 = (kv * WIN_GROUP + g) * WIN_HEAD_DIM
                o_ref[0, :, c0:c0 + WIN_HEAD_DIM] = o[g * QBLOCK:(g + 1) * QBLOCK].astype(o_ref.dtype)

    is_ctx = j < n_ctx // QBLOCK

    @pl.when(is_ctx)
    def _():
        run(False)

    @pl.when(jnp.logical_not(is_ctx))
    def _():
        run(True)


def _window_attention(l, n_ctx, sink, qw, kwt, vw):
    b, s, _ = qw.shape
    kern = functools.partial(_win_kernel, layer=l, n_ctx=n_ctx)
    return pl.pallas_call(
        kern,
        grid=(b, s // QBLOCK),
        in_specs=[
            pl.BlockSpec(memory_space=pltpu.SMEM),
            pl.BlockSpec((1, QBLOCK, WIN_WIDTH), lambda bi, j: (bi, j, 0)),
            pl.BlockSpec((1, LANES, s), lambda bi, j: (bi, 0, 0)),
            pl.BlockSpec((1, s, LANES), lambda bi, j: (bi, 0, 0)),
        ],
        out_specs=pl.BlockSpec((1, QBLOCK, WIN_WIDTH), lambda bi, j: (bi, j, 0)),
        out_shape=jax.ShapeDtypeStruct((b, s, WIN_WIDTH), BF16),
        compiler_params=_params("parallel", "parallel"),
        name="window_attention",
    )(sink, qw, kwt, vw)


def _s5_kernel(uf_ref, ub_ref, bb_ref, cc_ref, are_ref, aim_ref, yf_ref, yb_ref,
               st_ref, slab_ref, buf_ref, of_ref, ob_ref):
    @pl.when(pl.program_id(0) == 0)
    def _():
        st_ref[...] = jnp.zeros_like(st_ref)

    nb, steps, _ = uf_ref.shape
    rows = steps * SUBLANES
    ns = S5_BLOCK_STATES
    fwd_sub = lax.broadcasted_iota(jnp.int32, (SUBLANES, LANES), 0) < nb
    fwd = (lax.broadcasted_iota(jnp.int32, (rows, 1), 0) % SUBLANES) < nb
    for jb in range(S5_BLOCKS):
        lanes = slice(jb * LANES, (jb + 1) * LANES)
        for k in range(nb):
            slab_ref[jb, k * S5_PITCH:k * S5_PITCH + steps, :] = uf_ref[k, :, lanes]
            slab_ref[jb, (nb + k) * S5_PITCH:(nb + k) * S5_PITCH + steps, :] = ub_ref[k, :, lanes]
        lhs_f, lhs_b = [], []
        for i in range(steps):
            a = slab_ref[jb, pl.ds(i, SUBLANES, stride=S5_PITCH), :]
            b = slab_ref[jb, pl.ds(steps - 1 - i, SUBLANES, stride=S5_PITCH), :]
            lhs_f.append(jnp.where(fwd_sub, a, 0.0))
            lhs_b.append(jnp.where(fwd_sub, 0.0, b))
        lhs = jnp.concatenate([jnp.concatenate(lhs_f, axis=0).astype(BF16),
                               jnp.concatenate(lhs_b, axis=0).astype(BF16)], axis=1)
        buf = buf_ref.at[jb % 2]
        buf[...] = jnp.dot(lhs, bb_ref[0, jb], preferred_element_type=F32)
        a_re = are_ref[0, :, jb * ns:(jb + 1) * ns]
        a_im = aim_ref[0, :, jb * ns:(jb + 1) * ns]
        s_re = st_ref[0, :, jb * ns:(jb + 1) * ns]
        s_im = st_ref[1, :, jb * ns:(jb + 1) * ns]
        for t in range(steps):
            r = slice(t * SUBLANES, (t + 1) * SUBLANES)
            n_re = a_re * s_re - a_im * s_im + buf[r, 0:ns]
            n_im = a_re * s_im + a_im * s_re + buf[r, ns:2 * ns]
            buf[r, 0:ns] = n_re
            buf[r, ns:2 * ns] = n_im
            s_re, s_im = n_re, n_im
        st_ref[0, :, jb * ns:(jb + 1) * ns] = s_re
        st_ref[1, :, jb * ns:(jb + 1) * ns] = s_im
        y = jnp.dot(buf[...].astype(BF16), cc_ref[0, jb], preferred_element_type=F32)
        y = jnp.where(fwd, y[:, 0:LANES], y[:, LANES:2 * LANES])
        for i in range(steps):
            v = y[i * SUBLANES:(i + 1) * SUBLANES]
            of_ref[jb, pl.ds(i, SUBLANES, stride=S5_PITCH), :] = v
            ob_ref[jb, pl.ds(steps - 1 - i, SUBLANES, stride=S5_PITCH), :] = v
        for k in range(nb):
            yf_ref[k, :, lanes] = of_ref[jb, k * S5_PITCH:k * S5_PITCH + steps, :]
            yb_ref[k, :, lanes] = ob_ref[jb, (nb + k) * S5_PITCH:(nb + k) * S5_PITCH + steps, :]


def _s5_scan(l, n_ctx, su, bbcat, ccat, a_re, a_im):
    b, s, width = su.shape
    rows = S5_CHUNK * SUBLANES
    n_states = S5_GROUPS * S5_STATE
    n_tiles = s // S5_CHUNK
    ctx_tiles = n_ctx // S5_CHUNK
    mirror = lambda c: jnp.where(c < ctx_tiles, ctx_tiles - 1 - c, n_tiles + ctx_tiles - 1 - c)
    fwd_spec = pl.BlockSpec((b, S5_CHUNK, width), lambda c: (0, c, 0))
    bwd_spec = pl.BlockSpec((b, S5_CHUNK, width), lambda c: (0, mirror(c), 0))
    slab = pltpu.VMEM((width // LANES, SUBLANES * S5_PITCH, LANES), F32)
    return pl.pallas_call(
        _s5_kernel,
        grid=(n_tiles,),
        in_specs=[
            fwd_spec, bwd_spec,
            _const_spec((1, S5_BLOCKS, 2 * LANES, 2 * S5_BLOCK_STATES), lambda c: (l, 0, 0, 0)),
            _const_spec((1, S5_BLOCKS, 2 * S5_BLOCK_STATES, 2 * LANES), lambda c: (l, 0, 0, 0)),
            _const_spec((1, SUBLANES, n_states), lambda c: (l, 0, 0)),
            _const_spec((1, SUBLANES, n_states), lambda c: (l, 0, 0)),
        ],
        out_specs=(fwd_spec, bwd_spec),
        out_shape=(jax.ShapeDtypeStruct((b, s, width), F32), jax.ShapeDtypeStruct((b, s, width), F32)),
        scratch_shapes=[
            pltpu.VMEM((2, SUBLANES, n_states), F32),
            slab,
            pltpu.VMEM((2, rows, 2 * S5_BLOCK_STATES), F32),
            slab, slab,
        ],
        compiler_params=_params("arbitrary"),
        name="s5_scan",
    )(su, su, bbcat, ccat, a_re, a_im)


def _gelu_tanh(x):
    return 0.5 * x * (1.0 + jnp.tanh(math.sqrt(2.0 / math.pi) * (x + 0.044715 * (x * x * x))))


def _merge_kernel(h_ref, mod_ref, yd_ref, su_ref, yf_ref, yb_ref, yw_ref, gt_ref,
                  dsk_ref, wglu_ref, wd_ref, ws_ref, ww_ref, wo_ref, o_ref):
    d = h_ref.shape[2]
    y = su_ref[0] * dsk_ref[0] + yf_ref[0] + yb_ref[0]
    g = _gelu_tanh(y)
    ys = g * _sigmoid(jnp.dot(g.astype(BF16), wglu_ref[0], preferred_element_type=F32))
    m = (gt_ref[0, :, 0:d] * jnp.dot(yd_ref[0], wd_ref[0], preferred_element_type=F32)
         + gt_ref[0, :, d:2 * d] * jnp.dot(ys.astype(BF16), ws_ref[0], preferred_element_type=F32)
         + gt_ref[0, :, 2 * d:3 * d] * jnp.dot(yw_ref[0], ww_ref[0], preferred_element_type=F32))
    g1 = mod_ref[0, 0, 2:3]
    o_ref[0] = h_ref[0] + g1 * jnp.dot(m.astype(BF16), wo_ref[0], preferred_element_type=F32)


def _merge(l, h, mods, yd, su, yf, yb, yw, gates, dsk, wglu, wd, ws, ww, wo):
    b, s, d = h.shape
    nt = s // ROW_TILE
    mod_idx = lambda bi, j: (l, jnp.where(j == 0, b, bi), 0, 0)
    tile = lambda width: pl.BlockSpec((1, ROW_TILE, width), lambda bi, j: (bi, j, 0))
    wspec = lambda r, c: _const_spec((1, r, c), lambda bi, j: (l, 0, 0))
    return pl.pallas_call(
        _merge_kernel,
        grid=(b, nt),
        in_specs=[
            tile(d), pl.BlockSpec((1, 1, 6, d), mod_idx),
            tile(DIFF_WIDTH), tile(S5_WIDTH), tile(S5_WIDTH), tile(S5_WIDTH), tile(WIN_WIDTH), tile(3 * d),
            wspec(1, S5_WIDTH), wspec(S5_WIDTH, S5_WIDTH),
            wspec(DIFF_WIDTH, d), wspec(S5_WIDTH, d), wspec(WIN_WIDTH, d), wspec(d, d),
        ],
        out_specs=tile(d),
        out_shape=jax.ShapeDtypeStruct((b, s, d), F32),
        compiler_params=_params("parallel", "parallel"),
        name="merge",
    )(h, mods, yd, su, yf, yb, yw, gates, dsk, wglu, wd, ws, ww, wo)


def _ffn_kernel(h_ref, mod_ref, n2g_ref, w1_ref, w2_ref, o_ref, *, latent_only):
    def run():
        x = h_ref[0]
        m = mod_ref[0, 0]
        f = (_rms(x, n2g_ref[0]) * (1.0 + m[4:5]) + m[3:4]).astype(BF16)
        t = jnp.maximum(jnp.dot(f, w1_ref[0], preferred_element_type=F32), 0.0)
        t = (t * t).astype(BF16)
        o_ref[0] = x + m[5:6] * jnp.dot(t, w2_ref[0], preferred_element_type=F32)

    if latent_only:
        pl.when(pl.program_id(1) > 0)(run)
    else:
        run()


def _ffn(l, h, mods, n2g, w1, w2, latent_only):
    b, s, d = h.shape
    nt = s // ROW_TILE
    d_ff = w1.shape[-1]
    mod_idx = lambda bi, j: (l, jnp.where(j == 0, b, bi), 0, 0)
    tile = pl.BlockSpec((1, ROW_TILE, d), lambda bi, j: (bi, j, 0))
    if latent_only:
        out_spec = pl.BlockSpec((1, ROW_TILE, d), lambda bi, j: (bi, jnp.maximum(j - 1, 0), 0))
        out_rows = s - ROW_TILE
    else:
        out_spec, out_rows = tile, s
    return pl.pallas_call(
        functools.partial(_ffn_kernel, latent_only=latent_only),
        grid=(b, nt),
        in_specs=[
            tile, pl.BlockSpec((1, 1, 6, d), mod_idx),
            _const_spec((1, 1, d), lambda bi, j: (l, 0, 0)),
            _const_spec((1, d, d_ff), lambda bi, j: (l, 0, 0)),
            _const_spec((1, d_ff, d), lambda bi, j: (l, 0, 0)),
        ],
        out_specs=out_spec,
        out_shape=jax.ShapeDtypeStruct((b, out_rows, d), F32),
        compiler_params=_params("parallel", "arbitrary"),
        name="ffn",
    )(h, mods, n2g, w1, w2)


def _permute_inproj(w_in):
    depth, d, _ = w_in.shape
    qk = w_in[:, :, 0:1024].reshape(depth, d, 2, 2, DIFF_HEADS, DIFF_QK_DIM)
    qk = qk.transpose(0, 1, 2, 4, 3, 5).reshape(depth, d, 1024)
    parts = [qk, w_in[:, :, 2048:2816], w_in[:, :, 1024:1536], w_in[:, :, 1536:2048], w_in[:, :, 2816:]]
    return jnp.concatenate(parts, -1).astype(BF16)


def _rope_tables(n_ctx, n_lat):
    n_freq = ROPE_DIM // 4
    inv = ROPE_BASE ** (-jnp.arange(n_freq, dtype=F32) / n_freq)
    rows = n_lat // GRID_W
    r = jnp.repeat(jnp.arange(rows, dtype=F32), GRID_W)
    col = jnp.tile(jnp.arange(GRID_W, dtype=F32), rows)
    ang = jnp.concatenate([r[:, None] * inv, col[:, None] * inv], -1)
    cos, sin = jnp.cos(ang), jnp.sin(ang)
    cosf = jnp.concatenate([jnp.ones((n_ctx, LANES), F32), jnp.tile(cos, (1, 4))], 0)
    sinf = jnp.concatenate([jnp.zeros((n_ctx, LANES), F32),
                            jnp.tile(jnp.concatenate([-sin, sin], -1), (1, 2))], 0)
    return cosf, sinf


def _s5_disc_kernel(lre_ref, lim_ref, ldt_ref, bre_ref, bim_ref, are_ref, aim_ref, bbre_ref, bbim_ref):
    lre = lre_ref[...]
    lim = lim_ref[...]
    dt = jnp.exp(ldt_ref[...])
    mag = jnp.exp(lre * dt)
    ang = lim * dt
    a_re = mag * jnp.cos(ang)
    a_im = mag * jnp.sin(ang)
    den = lre * lre + lim * lim
    n_re = a_re - 1.0
    f_re = (n_re * lre + a_im * lim) / den
    f_im = (a_im * lre - n_re * lim) / den
    are_ref[...] = a_re
    aim_ref[...] = a_im
    b_re = bre_ref[...]
    b_im = bim_ref[...]
    bbre_ref[...] = f_re * b_re - f_im * b_im
    bbim_ref[...] = f_re * b_im + f_im * b_re


def _s5_discretize(lam_re, lam_im, log_dt, b_re, b_im):
    depth, two, g, p = lam_re.shape
    hg = b_re.shape[-1]
    r = depth * two * g
    row = lambda a: a.reshape(r, 1, p)
    bt = lambda a: a.reshape(r, p, hg).transpose(0, 2, 1)
    vec = jax.ShapeDtypeStruct((r, 1, p), F32)
    mat = jax.ShapeDtypeStruct((r, hg, p), F32)
    a_re, a_im, bb_re, bb_im = pl.pallas_call(
        _s5_disc_kernel, out_shape=(vec, vec, mat, mat), name="s5_discretize",
    )(row(lam_re), row(lam_im), log_dt.reshape(r, 1, 1), bt(b_re), bt(b_im))
    shape = (depth, two, g, hg, p)
    return (a_re.reshape(depth, two, g, p), a_im.reshape(depth, two, g, p),
            bb_re.reshape(shape), bb_im.reshape(shape))


def _s5_tables(lam_re, lam_im, log_dt, b_re, b_im, c_re, c_im):
    depth = lam_re.shape[0]
    a_re, a_im, bb_re, bb_im = _s5_discretize(lam_re, lam_im, log_dt, b_re, b_im)
    bb_re = bb_re.transpose(0, 1, 2, 4, 3)
    bb_im = bb_im.transpose(0, 1, 2, 4, 3)
    eye = jnp.eye(S5_BLOCK_GROUPS, dtype=F32)
    shp = (depth, 2, S5_BLOCKS, S5_BLOCK_GROUPS, S5_STATE, S5_GROUP)

    def in_block(bb):
        t = jnp.einsum('ldjgph,gk->ldjghkp', bb.reshape(shp), eye)
        return t.reshape(depth, 2, S5_BLOCKS, LANES, S5_BLOCK_STATES).transpose(0, 2, 1, 3, 4)

    bbcat = jnp.concatenate([in_block(bb_re), in_block(bb_im)], -1)
    bbcat = bbcat.reshape(depth, S5_BLOCKS, 2 * LANES, 2 * S5_BLOCK_STATES).astype(BF16)
    shc = (depth, 2, S5_BLOCKS, S5_BLOCK_GROUPS, S5_GROUP, S5_STATE)

    def out_block(c):
        t = jnp.einsum('ldjghp,gk->ljgpdkh', c.reshape(shc), eye)
        return t.reshape(depth, S5_BLOCKS, S5_BLOCK_STATES, 2 * LANES)

    ccat = jnp.concatenate([out_block(c_re), out_block(-c_im)], 2).astype(BF16)
    n_states = S5_GROUPS * S5_STATE
    rep = lambda a: jnp.repeat(a.reshape(depth, 2, n_states), SUBLANES // 2, axis=1)
    return bbcat, ccat, rep(a_re), rep(a_im)


def kernel(x, c, ctx, c_ctx, w_mod, b_mod, norm1_g, norm2_g, w_in, diff_q_norm_g, diff_k_norm_g, diff_lam_q1, diff_lam_k1, diff_lam_q2, diff_lam_k2, diff_out_norm_g, s5_lambda_re, s5_lambda_im, s5_log_dt, s5_b_re, s5_b_im, s5_c_re, s5_c_im, s5_d, s5_w_glu, win_q_norm_g, win_k_norm_g, win_sink, w_proj_diff, w_proj_s5, w_proj_win, w_out, w_ff1, w_ff2):
    b, n_lat, d = x.shape
    n_ctx = ctx.shape[1]
    s = n_ctx + n_lat
    depth = w_in.shape[0]
    assert n_ctx == ROW_TILE and n_lat % ROW_TILE == 0 and b == SUBLANES // 2

    cc = jnp.zeros((SUBLANES, d), F32).at[:b].set(c).at[b].set(c_ctx)
    mods = _modulation(cc, w_mod, b_mod).reshape(depth, SUBLANES, 6, d)

    w_in_p = _permute_inproj(w_in)
    scale_d = DIFF_QK_DIM ** -0.5 * LOG2E
    scale_w = WIN_HEAD_DIM ** -0.5 * LOG2E
    gtab = jnp.concatenate([
        jnp.tile(diff_q_norm_g * scale_d, (1, 8)), jnp.tile(diff_k_norm_g, (1, 8)),
        jnp.tile(win_q_norm_g * scale_w, (1, 8)), jnp.tile(win_k_norm_g, (1, 2))], -1)[:, None, :]
    cosf, sinf = _rope_tables(n_ctx, n_lat)
    lam_p = jnp.stack([diff_lam_q1, diff_lam_k1, diff_lam_q2, diff_lam_k2], 1)
    bbcat, ccat, a_re, a_im = _s5_tables(s5_lambda_re, s5_lambda_im, s5_log_dt, s5_b_re, s5_b_im,
                                         s5_c_re, s5_c_im)
    bf = lambda w: w.astype(BF16)
    wglu, wd, ws, ww, wo, w1, w2 = (bf(s5_w_glu), bf(w_proj_diff), bf(w_proj_s5), bf(w_proj_win),
                                    bf(w_out), bf(w_ff1), bf(w_ff2))
    n1g, n2g = norm1_g[:, None, :], norm2_g[:, None, :]
    og, dsk = diff_out_norm_g[:, None, :], s5_d[:, None, :]

    h = jnp.concatenate([ctx, x], axis=1)
    for l in range(depth):
        lam_init = 0.8 - 0.6 * math.exp(-0.3 * l)
        qd, kdt, qw, kwt, vd, vw, su, gates = _inproj(l, h, mods, n1g, w_in_p, gtab, cosf, sinf)
        yd = _diff_attention(l, lam_init, n_ctx, lam_p, og, qd, kdt, vd)
        yw = _window_attention(l, n_ctx, win_sink, qw, kwt, vw)
        yf, yb = _s5_scan(l, n_ctx, su, bbcat, ccat, a_re, a_im)
        h = _merge(l, h, mods, yd, su, yf, yb, yw, gates, dsk, wglu, wd, ws, ww, wo)
        h = _ffn(l, h, mods, n2g, w1, w2, latent_only=(l == depth - 1))
    return h
```

```python
import functools
import math

import numpy as np
import jax
import jax.numpy as jnp
from jax import lax
from jax.experimental import pallas as pl
from jax.experimental.pallas import tpu as pltpu

F32 = jnp.float32
BF16 = jnp.bfloat16

GRID_W = 64
QBLOCK = 128
DIFF_HEADS = 4
DIFF_QK_DIM = 64
DIFF_V_DIM = 128
DIFF_WIDTH = DIFF_HEADS * DIFF_V_DIM
S5_WIDTH = 512
S5_GROUP = 16
S5_GROUPS = 32
S5_STATE = 64
WIN_Q_HEADS = 8
WIN_KV_HEADS = 2
WIN_HEAD_DIM = 64
WIN_GROUP = WIN_Q_HEADS // WIN_KV_HEADS
WIN_WIDTH = WIN_Q_HEADS * WIN_HEAD_DIM
WINDOW = 128
ROPE_DIM = 64
ROPE_BASE = 10000.0
EPS = 1e-6
NEG_INF = -1e30

LANES = 128
SUBLANES = 8
ROW_TILE = 256
DIFF_HEADS_PER_STEP = 4
S5_CHUNK = 128
S5_BLOCK_GROUPS = LANES // S5_GROUP
S5_BLOCKS = S5_GROUPS // S5_BLOCK_GROUPS
S5_BLOCK_STATES = S5_BLOCK_GROUPS * S5_STATE
S5_PIECES = 4
S5_PITCH = S5_CHUNK + SUBLANES
LOG2E = math.log2(math.e)
VMEM_LIMIT = 48 * 1024 * 1024
MERGE_FFN_VMEM_LIMIT = 56 * 1024 * 1024

N_NORM_CHUNKS = 13
OFF_QD, OFF_KD, OFF_QW, OFF_KW = 0, 512, 1024, 1536
OFF_VW, OFF_VD, OFF_SU, OFF_GT = 1664, 1792, 2304, 2816


def _sigmoid(x):
    return 1.0 / (1.0 + jnp.exp(-x))


def _rms(x, g):
    return x * lax.rsqrt(jnp.mean(x * x, axis=-1, keepdims=True) + EPS) * g


def _const_spec(shape, index_map):
    return pl.BlockSpec(shape, index_map, pipeline_mode=pl.Buffered(1))


def _params(*sem):
    return pltpu.CompilerParams(dimension_semantics=sem, vmem_limit_bytes=VMEM_LIMIT)


def _mod_kernel(cc_ref, w_ref, b_ref, o_ref):
    cc = cc_ref[...]
    s = cc * _sigmoid(cc)
    o_ref[0] = jnp.dot(s.astype(BF16), w_ref[0].astype(BF16), preferred_element_type=F32) + b_ref[0]


def _modulation(cc, w_mod, b_mod):
    depth, d, d6 = w_mod.shape
    tn = 1536
    return pl.pallas_call(
        _mod_kernel,
        grid=(depth, d6 // tn),
        in_specs=[
            pl.BlockSpec((SUBLANES, d), lambda l, n: (0, 0)),
            pl.BlockSpec((1, d, tn), lambda l, n: (l, 0, n)),
            pl.BlockSpec((1, 1, tn), lambda l, n: (l, 0, n)),
        ],
        out_specs=pl.BlockSpec((1, SUBLANES, tn), lambda l, n: (l, 0, n)),
        out_shape=jax.ShapeDtypeStruct((depth, SUBLANES, d6), F32),
        compiler_params=_params("parallel", "parallel"),
        name="modulation",
    )(cc, w_mod, b_mod.reshape(depth, 1, d6))


def _inproj_kernel(h_ref, mod_ref, n1g_ref, w_ref, gtab_ref, cos_ref, sin_ref,
                   qd_ref, kdt_ref, qw_ref, kwt_ref, vd_ref, vw_ref, su_ref, gt_ref):
    x = h_ref[0]
    m = mod_ref[0, 0]
    a = (_rms(x, n1g_ref[0]) * (1.0 + m[1:2]) + m[0:1]).astype(BF16)
    tm = x.shape[0]
    lane = lax.broadcasted_iota(jnp.int32, (tm, LANES), 1)
    low_seg = lane < ROPE_DIM
    first_half = (lane % ROPE_DIM) < (ROPE_DIM // 2)
    cosf = cos_ref[...]
    sinf = sin_ref[...]
    for c in range(N_NORM_CHUNKS):
        if c % 2 == 0:
            pair = jnp.dot(a, w_ref[0, :, c * LANES:(c + 2) * LANES], preferred_element_type=F32)
        xc = pair[:, (c % 2) * LANES:(c % 2 + 1) * LANES]
        y = xc * xc
        ss_lo = jnp.sum(jnp.where(low_seg, y, 0.0), axis=-1, keepdims=True)
        ss_hi = jnp.sum(jnp.where(low_seg, 0.0, y), axis=-1, keepdims=True)
        r = jnp.where(low_seg, lax.rsqrt(ss_lo / ROPE_DIM + EPS), lax.rsqrt(ss_hi / ROPE_DIM + EPS))
        xn = xc * r * gtab_ref[0, :, c * LANES:(c + 1) * LANES]
        partner = jnp.where(first_half, pltpu.roll(xn, LANES - ROPE_DIM // 2, 1),
                            pltpu.roll(xn, ROPE_DIM // 2, 1))
        out = xn * cosf + partner * sinf
        if c < 4:
            qd_ref[0, :, c * LANES:(c + 1) * LANES] = out.astype(BF16)
        elif c < 8:
            kdt_ref[0, (c - 4) * LANES:(c - 3) * LANES, :] = out.T.astype(BF16)
        elif c < 12:
            qw_ref[0, :, (c - 8) * LANES:(c - 7) * LANES] = out.astype(BF16)
        else:
            kwt_ref[0] = out.T.astype(BF16)
    vw_ref[0] = pair[:, LANES:2 * LANES].astype(BF16)
    vd_ref[0] = jnp.dot(a, w_ref[0, :, OFF_VD:OFF_SU], preferred_element_type=F32).astype(BF16)
    su_ref[0] = jnp.dot(a, w_ref[0, :, OFF_SU:OFF_GT], preferred_element_type=F32)
    gt_ref[0] = _sigmoid(jnp.dot(a, w_ref[0, :, OFF_GT:], preferred_element_type=F32))


def _inproj(l, h, mods, n1g, w_in_p, gtab, cosf, sinf):
    b, s, d = h.shape
    d_in = w_in_p.shape[-1]
    nt = s // ROW_TILE
    mod_idx = lambda bi, j: (l, jnp.where(j == 0, b, bi), 0, 0)
    tile = lambda width: pl.BlockSpec((1, ROW_TILE, width), lambda bi, j: (bi, j, 0))
    out_shapes = (
        jax.ShapeDtypeStruct((b, s, 512), BF16),
        jax.ShapeDtypeStruct((b, 512, s), BF16),
        jax.ShapeDtypeStruct((b, s, WIN_WIDTH), BF16),
        jax.ShapeDtypeStruct((b, LANES, s), BF16),
        jax.ShapeDtypeStruct((b, s, DIFF_WIDTH), BF16),
        jax.ShapeDtypeStruct((b, s, LANES), BF16),
        jax.ShapeDtypeStruct((b, s, S5_WIDTH), F32),
        jax.ShapeDtypeStruct((b, s, 3 * d), F32),
    )
    tile_t = lambda width: pl.BlockSpec((1, width, ROW_TILE), lambda bi, j: (bi, 0, j))
    out_specs = (
        tile(512), tile_t(512), tile(WIN_WIDTH), tile_t(LANES),
        tile(DIFF_WIDTH), tile(LANES), tile(S5_WIDTH), tile(3 * d),
    )
    return pl.pallas_call(
        _inproj_kernel,
        grid=(b, nt),
        in_specs=[
            tile(d),
            pl.BlockSpec((1, 1, 6, d), mod_idx),
            _const_spec((1, 1, d), lambda bi, j: (l, 0, 0)),
            _const_spec((1, d, d_in), lambda bi, j: (l, 0, 0)),
            _const_spec((1, 1, N_NORM_CHUNKS * LANES), lambda bi, j: (l, 0, 0)),
            pl.BlockSpec((ROW_TILE, LANES), lambda bi, j: (j, 0)),
            pl.BlockSpec((ROW_TILE, LANES), lambda bi, j: (j, 0)),
        ],
        out_specs=out_specs,
        out_shape=out_shapes,
        compiler_params=_params("parallel", "parallel"),
        name="inproj",
    )(h, mods, n1g, w_in_p, gtab, cosf, sinf)


def _diff_kernel(lam_ref, og_ref, q_ref, kt_ref, v_ref, o_ref, *, lam_init, n_ctx):
    lp = lam_ref[0]
    lam = (jnp.exp(jnp.sum(lp[0:1] * lp[1:2], axis=-1, keepdims=True))
           - jnp.exp(jnp.sum(lp[2:3] * lp[3:4], axis=-1, keepdims=True)) + lam_init)
    og = og_ref[0]
    heads = q_ref.shape[2] // LANES

    def attend(n_keys):
        for hd in range(heads):
            v = v_ref[0, 0:n_keys, hd * DIFF_V_DIM:(hd + 1) * DIFF_V_DIM]
            v1 = jnp.concatenate([v, jnp.ones_like(v)], axis=1)
            outs = []
            for m in range(2):
                c0 = hd * LANES + m * DIFF_QK_DIM
                qm = q_ref[0, :, c0:c0 + DIFF_QK_DIM]
                km = kt_ref[0, c0:c0 + DIFF_QK_DIM, 0:n_keys]
                s = jnp.dot(qm, km, preferred_element_type=F32)
                e = jnp.exp2(s - jnp.max(s, axis=-1, keepdims=True))
                eb = e.astype(BF16)
                half = eb.shape[0] // 2
                pv = jnp.concatenate([jnp.dot(eb[0:half], v1, preferred_element_type=F32),
                                      jnp.dot(eb[half:], v1, preferred_element_type=F32)], axis=0)
                outs.append(pv[:, 0:DIFF_V_DIM] / pv[:, DIFF_V_DIM:2 * DIFF_V_DIM])
            o = outs[0] - lam * outs[1]
            o_ref[0, :, hd * DIFF_V_DIM:(hd + 1) * DIFF_V_DIM] = (
                _rms(o, og) * (1.0 - lam_init)).astype(o_ref.dtype)

    is_ctx = pl.program_id(2) == 0

    @pl.when(is_ctx)
    def _():
        attend(n_ctx)

    @pl.when(jnp.logical_not(is_ctx))
    def _():
        attend(kt_ref.shape[2])


def _diff_attention(l, lam_init, n_ctx, lam_p, out_g, qd, kdt, vd):
    b, s, _ = qd.shape
    nt = s // ROW_TILE
    hw = DIFF_HEADS_PER_STEP * LANES
    kern = functools.partial(_diff_kernel, lam_init=lam_init, n_ctx=n_ctx)
    return pl.pallas_call(
        kern,
        grid=(b, DIFF_HEADS // DIFF_HEADS_PER_STEP, nt),
        in_specs=[
            _const_spec((1, 4, DIFF_QK_DIM), lambda bi, hd, j: (l, 0, 0)),
            _const_spec((1, 1, DIFF_V_DIM), lambda bi, hd, j: (l, 0, 0)),
            pl.BlockSpec((1, ROW_TILE, hw), lambda bi, hd, j: (bi, j, hd)),
            pl.BlockSpec((1, hw, s), lambda bi, hd, j: (bi, hd, 0)),
            pl.BlockSpec((1, s, hw), lambda bi, hd, j: (bi, 0, hd)),
        ],
        out_specs=pl.BlockSpec((1, ROW_TILE, hw), lambda bi, hd, j: (bi, j, hd)),
        out_shape=jax.ShapeDtypeStruct((b, s, DIFF_WIDTH), BF16),
        compiler_params=_params("parallel", "parallel", "parallel"),
        name="diff_attention",
    )(lam_p, out_g, qd, kdt, vd)


def _win_kernel(sink_ref, q_ref, kt_ref, v_ref, o_ref, *, layer, n_ctx):
    s_total = kt_ref.shape[2]
    band = 3 * QBLOCK
    rows = WIN_GROUP * QBLOCK
    row_group = lax.broadcasted_iota(jnp.int32, (rows, 1), 0) // QBLOCK
    blocks_per_step = q_ref.shape[1] // QBLOCK

    def run(with_band, sub):
        j = pl.program_id(1) * blocks_per_step + sub
        q = q_ref[0, sub * QBLOCK:(sub + 1) * QBLOCK, :]
        if with_band:
            start = pl.multiple_of(QBLOCK * jnp.minimum(j - 1, (s_total - band) // QBLOCK), QBLOCK)
            qpos = QBLOCK * j + lax.broadcasted_iota(jnp.int32, (QBLOCK, band), 0)
            kpos = start + lax.broadcasted_iota(jnp.int32, (QBLOCK, band), 1)
            valid = jnp.where(kpos >= n_ctx, jnp.abs(kpos - qpos), WINDOW + 1) <= WINDOW
            bias = jnp.concatenate([jnp.where(valid, 0.0, NEG_INF)] * WIN_GROUP, axis=0)
            v_b = v_ref[0, pl.ds(start, band), :]
            vb1 = jnp.concatenate([v_b, jnp.ones_like(v_b)], axis=1)
        v_c = v_ref[0, 0:n_ctx, :]
        vc1 = jnp.concatenate([v_c, jnp.ones_like(v_c)], axis=1)
        for kv in range(WIN_KV_HEADS):
            q4 = jnp.concatenate(
                [q[:, (kv * WIN_GROUP + g) * WIN_HEAD_DIM:(kv * WIN_GROUP + g + 1) * WIN_HEAD_DIM]
                 for g in range(WIN_GROUP)], axis=0)
            sink = jnp.zeros((rows, 1), F32)
            for g in range(WIN_GROUP):
                sink = jnp.where(row_group == g, sink_ref[layer, kv * WIN_GROUP + g] * LOG2E, sink)
            ksl = slice(kv * WIN_HEAD_DIM, (kv + 1) * WIN_HEAD_DIM)
            s_c = jnp.dot(q4, kt_ref[0, ksl, 0:n_ctx], preferred_element_type=F32)
            mx = jnp.maximum(jnp.max(s_c, axis=-1, keepdims=True), sink)
            if with_band:
                s_b = jnp.dot(q4, kt_ref[0, ksl, pl.ds(start, band)], preferred_element_type=F32) + bias
                mx = jnp.maximum(mx, jnp.max(s_b, axis=-1, keepdims=True))
            pv = jnp.dot(jnp.exp2(s_c - mx).astype(BF16), vc1, preferred_element_type=F32)
            if with_band:
                pv = pv + jnp.dot(jnp.exp2(s_b - mx).astype(BF16), vb1, preferred_element_type=F32)
            den = pv[:, LANES:LANES + WIN_HEAD_DIM] + jnp.exp2(sink - mx)
            o = pv[:, ksl] / den
            for g in range(WIN_GROUP):
                c0 = (kv * WIN_GROUP + g) * WIN_HEAD_DIM
                o_ref[0, sub * QBLOCK:(sub + 1) * QBLOCK, c0:c0 + WIN_HEAD_DIM] = (
                    o[g * QBLOCK:(g + 1) * QBLOCK].astype(o_ref.dtype))

    is_ctx = pl.program_id(1) < n_ctx // (QBLOCK * blocks_per_step)

    @pl.when(is_ctx)
    def _():
        for sub in range(blocks_per_step):
            run(False, sub)

    @pl.when(jnp.logical_not(is_ctx))
    def _():
        for sub in range(blocks_per_step):
            run(True, sub)


def _window_attention(l, n_ctx, sink, qw, kwt, vw):
    b, s, _ = qw.shape
    kern = functools.partial(_win_kernel, layer=l, n_ctx=n_ctx)
    return pl.pallas_call(
        kern,
        grid=(b, s // ROW_TILE),
        in_specs=[
            pl.BlockSpec(memory_space=pltpu.SMEM),
            pl.BlockSpec((1, ROW_TILE, WIN_WIDTH), lambda bi, j: (bi, j, 0)),
            pl.BlockSpec((1, LANES, s), lambda bi, j: (bi, 0, 0)),
            pl.BlockSpec((1, s, LANES), lambda bi, j: (bi, 0, 0)),
        ],
        out_specs=pl.BlockSpec((1, ROW_TILE, WIN_WIDTH), lambda bi, j: (bi, j, 0)),
        out_shape=jax.ShapeDtypeStruct((b, s, WIN_WIDTH), BF16),
        compiler_params=_params("parallel", "parallel"),
        name="window_attention",
    )(sink, qw, kwt, vw)


def _s5_kernel(uf_ref, ub_ref, bb_ref, cc_ref, are_ref, aim_ref, yf_ref, yb_ref,
               st_ref, slab_ref, buf_ref, of_ref, ob_ref):
    @pl.when(pl.program_id(0) == 0)
    def _():
        st_ref[...] = jnp.zeros_like(st_ref)

    nb, steps, _ = uf_ref.shape
    rows = steps * SUBLANES
    ns = S5_BLOCK_STATES
    fwd_sub = lax.broadcasted_iota(jnp.int32, (SUBLANES, LANES), 0) < nb
    fwd = (lax.broadcasted_iota(jnp.int32, (rows, 1), 0) % SUBLANES) < nb
    for jb in range(S5_BLOCKS):
        lanes = slice(jb * LANES, (jb + 1) * LANES)
        for k in range(nb):
            slab_ref[jb, k * S5_PITCH:k * S5_PITCH + steps, :] = uf_ref[k, :, lanes]
            slab_ref[jb, (nb + k) * S5_PITCH:(nb + k) * S5_PITCH + steps, :] = ub_ref[k, :, lanes]
        lhs_f, lhs_b = [], []
        for i in range(steps):
            a = slab_ref[jb, pl.ds(i, SUBLANES, stride=S5_PITCH), :]
            b = slab_ref[jb, pl.ds(steps - 1 - i, SUBLANES, stride=S5_PITCH), :]
            lhs_f.append(jnp.where(fwd_sub, a, 0.0))
            lhs_b.append(jnp.where(fwd_sub, 0.0, b))
        buf = buf_ref.at[jb % 2]
        piece = steps // S5_PIECES
        for p in range(S5_PIECES):
            sl = slice(p * piece, (p + 1) * piece)
            lhs = jnp.concatenate([jnp.concatenate(lhs_f[sl], axis=0).astype(BF16),
                                   jnp.concatenate(lhs_b[sl], axis=0).astype(BF16)], axis=1)
            buf[p * piece * SUBLANES:(p + 1) * piece * SUBLANES, :] = jnp.dot(
                lhs, bb_ref[0, jb], preferred_element_type=F32)
        a_re = are_ref[0, :, jb * ns:(jb + 1) * ns]
        a_im = aim_ref[0, :, jb * ns:(jb + 1) * ns]
        s_re = st_ref[0, :, jb * ns:(jb + 1) * ns]
        s_im = st_ref[1, :, jb * ns:(jb + 1) * ns]
        for t in range(steps):
            r = slice(t * SUBLANES, (t + 1) * SUBLANES)
            n_re = a_re * s_re - a_im * s_im + buf[r, 0:ns]
            n_im = a_re * s_im + a_im * s_re + buf[r, ns:2 * ns]
            buf[r, 0:ns] = n_re
            buf[r, ns:2 * ns] = n_im
            s_re, s_im = n_re, n_im
        st_ref[0, :, jb * ns:(jb + 1) * ns] = s_re
        st_ref[1, :, jb * ns:(jb + 1) * ns] = s_im
        for p in range(S5_PIECES):
            r0 = p * piece * SUBLANES
            y = jnp.dot(buf[r0:r0 + piece * SUBLANES, :].astype(BF16), cc_ref[0, jb],
                        preferred_element_type=F32)
            y = jnp.where(fwd[0:piece * SUBLANES], y[:, 0:LANES], y[:, LANES:2 * LANES])
            for i in range(piece):
                v = y[i * SUBLANES:(i + 1) * SUBLANES]
                t = p * piece + i
                of_ref[jb, pl.ds(t, SUBLANES, stride=S5_PITCH), :] = v
                ob_ref[jb, pl.ds(steps - 1 - t, SUBLANES, stride=S5_PITCH), :] = v
        for k in range(nb):
            yf_ref[k, :, lanes] = of_ref[jb, k * S5_PITCH:k * S5_PITCH + steps, :]
            yb_ref[k, :, lanes] = ob_ref[jb, (nb + k) * S5_PITCH:(nb + k) * S5_PITCH + steps, :]


def _s5_scan(l, n_ctx, su, bbcat, ccat, a_re, a_im):
    b, s, width = su.shape
    rows = S5_CHUNK * SUBLANES
    n_states = S5_GROUPS * S5_STATE
    n_tiles = s // S5_CHUNK
    ctx_tiles = n_ctx // S5_CHUNK
    mirror = lambda c: jnp.where(c < ctx_tiles, ctx_tiles - 1 - c, n_tiles + ctx_tiles - 1 - c)
    fwd_spec = pl.BlockSpec((b, S5_CHUNK, width), lambda c: (0, c, 0))
    bwd_spec = pl.BlockSpec((b, S5_CHUNK, width), lambda c: (0, mirror(c), 0))
    slab = pltpu.VMEM((width // LANES, SUBLANES * S5_PITCH, LANES), F32)
    return pl.pallas_call(
        _s5_kernel,
        grid=(n_tiles,),
        in_specs=[
            fwd_spec, bwd_spec,
            _const_spec((1, S5_BLOCKS, 2 * LANES, 2 * S5_BLOCK_STATES), lambda c: (l, 0, 0, 0)),
            _const_spec((1, S5_BLOCKS, 2 * S5_BLOCK_STATES, 2 * LANES), lambda c: (l, 0, 0, 0)),
            _const_spec((1, SUBLANES, n_states), lambda c: (l, 0, 0)),
            _const_spec((1, SUBLANES, n_states), lambda c: (l, 0, 0)),
        ],
        out_specs=(fwd_spec, bwd_spec),
        out_shape=(jax.ShapeDtypeStruct((b, s, width), F32), jax.ShapeDtypeStruct((b, s, width), F32)),
        scratch_shapes=[
            pltpu.VMEM((2, SUBLANES, n_states), F32),
            slab,
            pltpu.VMEM((2, rows, 2 * S5_BLOCK_STATES), F32),
            slab, slab,
        ],
        compiler_params=_params("arbitrary"),
        name="s5_scan",
    )(su, su, bbcat, ccat, a_re, a_im)


def _gelu_tanh(x):
    return 0.5 * x * (1.0 + jnp.tanh(math.sqrt(2.0 / math.pi) * (x + 0.044715 * (x * x * x))))


def _merge_ffn_kernel(h_ref, mod_ref, yd_ref, su_ref, yf_ref, yb_ref, yw_ref, gt_ref,
                      dsk_ref, wglu_ref, wd_ref, ws_ref, ww_ref, wo_ref, n2g_ref, w1_ref, w2_ref,
                      o_ref, *, latent_only):
    def run():
        d = h_ref.shape[2]
        mod = mod_ref[0, 0]
        y = su_ref[0] * dsk_ref[0] + yf_ref[0] + yb_ref[0]
        g = _gelu_tanh(y)
        ys = g * _sigmoid(jnp.dot(g.astype(BF16), wglu_ref[0], preferred_element_type=F32))
        m = (gt_ref[0, :, 0:d] * jnp.dot(yd_ref[0], wd_ref[0], preferred_element_type=F32)
             + gt_ref[0, :, d:2 * d] * jnp.dot(ys.astype(BF16), ws_ref[0], preferred_element_type=F32)
             + gt_ref[0, :, 2 * d:3 * d] * jnp.dot(yw_ref[0], ww_ref[0], preferred_element_type=F32))
        x = h_ref[0] + mod[2:3] * jnp.dot(m.astype(BF16), wo_ref[0], preferred_element_type=F32)
        f = (_rms(x, n2g_ref[0]) * (1.0 + mod[4:5]) + mod[3:4]).astype(BF16)
        t = jnp.maximum(jnp.dot(f, w1_ref[0], preferred_element_type=F32), 0.0)
        t = (t * t).astype(BF16)
        o_ref[0] = x + mod[5:6] * jnp.dot(t, w2_ref[0], preferred_element_type=F32)

    if latent_only:
        pl.when(pl.program_id(1) > 0)(run)
    else:
        run()


def _merge_ffn(l, h, mods, yd, su, yf, yb, yw, gates, dsk, wglu, wd, ws, ww, wo, n2g, w1, w2, latent_only):
    b, s, d = h.shape
    nt = s // ROW_TILE
    d_ff = w1.shape[-1]
    mod_idx = lambda bi, j: (l, jnp.where(j == 0, b, bi), 0, 0)
    tile = lambda width: pl.BlockSpec((1, ROW_TILE, width), lambda bi, j: (bi, j, 0))
    wspec = lambda r, c: _const_spec((1, r, c), lambda bi, j: (l, 0, 0))
    if latent_only:
        out_spec = pl.BlockSpec((1, ROW_TILE, d), lambda bi, j: (bi, jnp.maximum(j - 1, 0), 0))
        out_rows = s - ROW_TILE
    else:
        out_spec, out_rows = tile(d), s
    return pl.pallas_call(
        functools.partial(_merge_ffn_kernel, latent_only=latent_only),
        grid=(b, nt),
        in_specs=[
            tile(d), pl.BlockSpec((1, 1, 6, d), mod_idx),
            tile(DIFF_WIDTH), tile(S5_WIDTH), tile(S5_WIDTH), tile(S5_WIDTH), tile(WIN_WIDTH), tile(3 * d),
            wspec(1, S5_WIDTH), wspec(S5_WIDTH, S5_WIDTH),
            wspec(DIFF_WIDTH, d), wspec(S5_WIDTH, d), wspec(WIN_WIDTH, d), wspec(d, d),
            wspec(1, d), wspec(d, d_ff), wspec(d_ff, d),
        ],
        out_specs=out_spec,
        out_shape=jax.ShapeDtypeStruct((b, out_rows, d), F32),
        compiler_params=pltpu.CompilerParams(dimension_semantics=("parallel", "arbitrary"),
                                             vmem_limit_bytes=MERGE_FFN_VMEM_LIMIT),
        name="merge_ffn",
    )(h, mods, yd, su, yf, yb, yw, gates, dsk, wglu, wd, ws, ww, wo, n2g, w1, w2)


def _permute_inproj(w_in):
    depth, d, _ = w_in.shape
    qk = w_in[:, :, 0:1024].reshape(depth, d, 2, 2, DIFF_HEADS, DIFF_QK_DIM)
    qk = qk.transpose(0, 1, 2, 4, 3, 5).reshape(depth, d, 1024)
    parts = [qk, w_in[:, :, 2048:2816], w_in[:, :, 1024:1536], w_in[:, :, 1536:2048], w_in[:, :, 2816:]]
    return jnp.concatenate(parts, -1).astype(BF16)


def _rope_tables(n_ctx, n_lat):
    n_freq = ROPE_DIM // 4
    inv = ROPE_BASE ** (-jnp.arange(n_freq, dtype=F32) / n_freq)
    rows = n_lat // GRID_W
    r = jnp.repeat(jnp.arange(rows, dtype=F32), GRID_W)
    col = jnp.tile(jnp.arange(GRID_W, dtype=F32), rows)
    ang = jnp.concatenate([r[:, None] * inv, col[:, None] * inv], -1)
    cos, sin = jnp.cos(ang), jnp.sin(ang)
    cosf = jnp.concatenate([jnp.ones((n_ctx, LANES), F32), jnp.tile(cos, (1, 4))], 0)
    sinf = jnp.concatenate([jnp.zeros((n_ctx, LANES), F32),
                            jnp.tile(jnp.concatenate([-sin, sin], -1), (1, 2))], 0)
    return cosf, sinf


def _s5_disc_kernel(lre_ref, lim_ref, ldt_ref, bre_ref, bim_ref, are_ref, aim_ref, bbre_ref, bbim_ref):
    lre = lre_ref[...]
    lim = lim_ref[...]
    dt = jnp.exp(ldt_ref[...])
    mag = jnp.exp(lre * dt)
    ang = lim * dt
    a_re = mag * jnp.cos(ang)
    a_im = mag * jnp.sin(ang)
    den = lre * lre + lim * lim
    n_re = a_re - 1.0
    f_re = (n_re * lre + a_im * lim) / den
    f_im = (a_im * lre - n_re * lim) / den
    are_ref[...] = a_re
    aim_ref[...] = a_im
    b_re = bre_ref[...]
    b_im = bim_ref[...]
    bbre_ref[...] = f_re * b_re - f_im * b_im
    bbim_ref[...] = f_re * b_im + f_im * b_re


def _s5_discretize(lam_re, lam_im, log_dt, b_re, b_im):
    depth, two, g, p = lam_re.shape
    hg = b_re.shape[-1]
    r = depth * two * g
    row = lambda a: a.reshape(r, 1, p)
    bt = lambda a: a.reshape(r, p, hg).transpose(0, 2, 1)
    vec = jax.ShapeDtypeStruct((r, 1, p), F32)
    mat = jax.ShapeDtypeStruct((r, hg, p), F32)
    a_re, a_im, bb_re, bb_im = pl.pallas_call(
        _s5_disc_kernel, out_shape=(vec, vec, mat, mat), name="s5_discretize",
    )(row(lam_re), row(lam_im), log_dt.reshape(r, 1, 1), bt(b_re), bt(b_im))
    shape = (depth, two, g, hg, p)
    return (a_re.reshape(depth, two, g, p), a_im.reshape(depth, two, g, p),
            bb_re.reshape(shape), bb_im.reshape(shape))


def _s5_tables(lam_re, lam_im, log_dt, b_re, b_im, c_re, c_im):
    depth = lam_re.shape[0]
    a_re, a_im, bb_re, bb_im = _s5_discretize(lam_re, lam_im, log_dt, b_re, b_im)
    bb_re = bb_re.transpose(0, 1, 2, 4, 3)
    bb_im = bb_im.transpose(0, 1, 2, 4, 3)
    eye = jnp.eye(S5_BLOCK_GROUPS, dtype=F32)
    shp = (depth, 2, S5_BLOCKS, S5_BLOCK_GROUPS, S5_STATE, S5_GROUP)

    def in_block(bb):
        t = jnp.einsum('ldjgph,gk->ldjghkp', bb.reshape(shp), eye)
        return t.reshape(depth, 2, S5_BLOCKS, LANES, S5_BLOCK_STATES).transpose(0, 2, 1, 3, 4)

    bbcat = jnp.concatenate([in_block(bb_re), in_block(bb_im)], -1)
    bbcat = bbcat.reshape(depth, S5_BLOCKS, 2 * LANES, 2 * S5_BLOCK_STATES).astype(BF16)
    shc = (depth, 2, S5_BLOCKS, S5_BLOCK_GROUPS, S5_GROUP, S5_STATE)

    def out_block(c):
        t = jnp.einsum('ldjghp,gk->ljgpdkh', c.reshape(shc), eye)
        return t.reshape(depth, S5_BLOCKS, S5_BLOCK_STATES, 2 * LANES)

    ccat = jnp.concatenate([out_block(c_re), out_block(-c_im)], 2).astype(BF16)
    n_states = S5_GROUPS * S5_STATE
    rep = lambda a: jnp.repeat(a.reshape(depth, 2, n_states), SUBLANES // 2, axis=1)
    return bbcat, ccat, rep(a_re), rep(a_im)


def kernel(x, c, ctx, c_ctx, w_mod, b_mod, norm1_g, norm2_g, w_in, diff_q_norm_g, diff_k_norm_g, diff_lam_q1, diff_lam_k1, diff_lam_q2, diff_lam_k2, diff_out_norm_g, s5_lambda_re, s5_lambda_im, s5_log_dt, s5_b_re, s5_b_im, s5_c_re, s5_c_im, s5_d, s5_w_glu, win_q_norm_g, win_k_norm_g, win_sink, w_proj_diff, w_proj_s5, w_proj_win, w_out, w_ff1, w_ff2):
    b, n_lat, d = x.shape
    n_ctx = ctx.shape[1]
    s = n_ctx + n_lat
    depth = w_in.shape[0]
    assert n_ctx == ROW_TILE and n_lat % ROW_TILE == 0 and b == SUBLANES // 2

    cc = jnp.zeros((SUBLANES, d), F32).at[:b].set(c).at[b].set(c_ctx)
    mods = _modulation(cc, w_mod, b_mod).reshape(depth, SUBLANES, 6, d)

    w_in_p = _permute_inproj(w_in)
    scale_d = DIFF_QK_DIM ** -0.5 * LOG2E
    scale_w = WIN_HEAD_DIM ** -0.5 * LOG2E
    gtab = jnp.concatenate([
        jnp.tile(diff_q_norm_g * scale_d, (1, 8)), jnp.tile(diff_k_norm_g, (1, 8)),
        jnp.tile(win_q_norm_g * scale_w, (1, 8)), jnp.tile(win_k_norm_g, (1, 2))], -1)[:, None, :]
    cosf, sinf = _rope_tables(n_ctx, n_lat)
    lam_p = jnp.stack([diff_lam_q1, diff_lam_k1, diff_lam_q2, diff_lam_k2], 1)
    bbcat, ccat, a_re, a_im = _s5_tables(s5_lambda_re, s5_lambda_im, s5_log_dt, s5_b_re, s5_b_im,
                                         s5_c_re, s5_c_im)
    bf = lambda w: w.astype(BF16)
    wglu, wd, ws, ww, wo, w1, w2 = (bf(s5_w_glu), bf(w_proj_diff), bf(w_proj_s5), bf(w_proj_win),
                                    bf(w_out), bf(w_ff1), bf(w_ff2))
    n1g, n2g = norm1_g[:, None, :], norm2_g[:, None, :]
    og, dsk = diff_out_norm_g[:, None, :], s5_d[:, None, :]

    h = jnp.concatenate([ctx, x], axis=1)
    for l in range(depth):
        lam_init = 0.8 - 0.6 * math.exp(-0.3 * l)
        qd, kdt, qw, kwt, vd, vw, su, gates = _inproj(l, h, mods, n1g, w_in_p, gtab, cosf, sinf)
        yd = _diff_attention(l, lam_init, n_ctx, lam_p, og, qd, kdt, vd)
        yw = _window_attention(l, n_ctx, win_sink, qw, kwt, vw)
        yf, yb = _s5_scan(l, n_ctx, su, bbcat, ccat, a_re, a_im)
        h = _merge_ffn(l, h, mods, yd, su, yf, yb, yw, gates, dsk, wglu, wd, ws, ww, wo, n2g, w1, w2,
                       latent_only=(l == depth - 1))
    return h
```

```python
import functools
import math

import numpy as np
import jax
import jax.numpy as jnp
from jax import lax
from jax.experimental import pallas as pl
from jax.experimental.pallas import tpu as pltpu

F32 = jnp.float32
BF16 = jnp.bfloat16

GRID_W = 64
QBLOCK = 128
DIFF_HEADS = 4
DIFF_QK_DIM = 64
DIFF_V_DIM = 128
DIFF_WIDTH = DIFF_HEADS * DIFF_V_DIM
S5_WIDTH = 512
S5_GROUP = 16
S5_GROUPS = 32
S5_STATE = 64
WIN_Q_HEADS = 8
WIN_KV_HEADS = 2
WIN_HEAD_DIM = 64
WIN_GROUP = WIN_Q_HEADS // WIN_KV_HEADS
WIN_WIDTH = WIN_Q_HEADS * WIN_HEAD_DIM
WINDOW = 128
ROPE_DIM = 64
ROPE_BASE = 10000.0
EPS = 1e-6
NEG_INF = -1e30

LANES = 128
SUBLANES = 8
ROW_TILE = 256
DIFF_HEADS_PER_STEP = 4
S5_CHUNK = 128
S5_BLOCK_GROUPS = LANES // S5_GROUP
S5_BLOCKS = S5_GROUPS // S5_BLOCK_GROUPS
S5_BLOCK_STATES = S5_BLOCK_GROUPS * S5_STATE
S5_PIECES = 4
S5_PITCH = S5_CHUNK + SUBLANES
LOG2E = math.log2(math.e)
VMEM_LIMIT = 48 * 1024 * 1024
MERGE_FFN_VMEM_LIMIT = 56 * 1024 * 1024

N_NORM_CHUNKS = 13
OFF_QD, OFF_KD, OFF_QW, OFF_KW = 0, 512, 1024, 1536
OFF_VW, OFF_VD, OFF_SU, OFF_GT = 1664, 1792, 2304, 2816


def _sigmoid(x):
    return 1.0 / (1.0 + jnp.exp(-x))


def _rms(x, g):
    return x * lax.rsqrt(jnp.mean(x * x, axis=-1, keepdims=True) + EPS) * g


def _const_spec(shape, index_map):
    return pl.BlockSpec(shape, index_map, pipeline_mode=pl.Buffered(1))


def _params(*sem):
    return pltpu.CompilerParams(dimension_semantics=sem, vmem_limit_bytes=VMEM_LIMIT)


def _mod_kernel(cc_ref, w_ref, b_ref, o_ref):
    cc = cc_ref[...]
    s = cc * _sigmoid(cc)
    o_ref[0] = jnp.dot(s.astype(BF16), w_ref[0].astype(BF16), preferred_element_type=F32) + b_ref[0]


def _modulation(cc, w_mod, b_mod):
    depth, d, d6 = w_mod.shape
    tn = 1536
    return pl.pallas_call(
        _mod_kernel,
        grid=(depth, d6 // tn),
        in_specs=[
            pl.BlockSpec((SUBLANES, d), lambda l, n: (0, 0)),
            pl.BlockSpec((1, d, tn), lambda l, n: (l, 0, n)),
            pl.BlockSpec((1, 1, tn), lambda l, n: (l, 0, n)),
        ],
        out_specs=pl.BlockSpec((1, SUBLANES, tn), lambda l, n: (l, 0, n)),
        out_shape=jax.ShapeDtypeStruct((depth, SUBLANES, d6), F32),
        compiler_params=_params("parallel", "parallel"),
        name="modulation",
    )(cc, w_mod, b_mod.reshape(depth, 1, d6))


def _inproj_kernel(h_ref, mod_ref, n1g_ref, w_ref, gtab_ref, cos_ref, sin_ref,
                   qd_ref, kdt_ref, qw_ref, kwt_ref, vd_ref, vw_ref, su_ref, gt_ref):
    x = h_ref[0]
    m = mod_ref[0, 0]
    a = (_rms(x, n1g_ref[0]) * (1.0 + m[1:2]) + m[0:1]).astype(BF16)
    tm = x.shape[0]
    lane = lax.broadcasted_iota(jnp.int32, (tm, LANES), 1)
    low_seg = lane < ROPE_DIM
    first_half = (lane % ROPE_DIM) < (ROPE_DIM // 2)
    cosf = cos_ref[...]
    sinf = sin_ref[...]
    for c in range(N_NORM_CHUNKS):
        if c % 2 == 0:
            pair = jnp.dot(a, w_ref[0, :, c * LANES:(c + 2) * LANES], preferred_element_type=F32)
        xc = pair[:, (c % 2) * LANES:(c % 2 + 1) * LANES]
        y = xc * xc
        ss_lo = jnp.sum(jnp.where(low_seg, y, 0.0), axis=-1, keepdims=True)
        ss_hi = jnp.sum(jnp.where(low_seg, 0.0, y), axis=-1, keepdims=True)
        r = jnp.where(low_seg, lax.rsqrt(ss_lo / ROPE_DIM + EPS), lax.rsqrt(ss_hi / ROPE_DIM + EPS))
        xn = xc * r * gtab_ref[0, :, c * LANES:(c + 1) * LANES]
        partner = jnp.where(first_half, pltpu.roll(xn, LANES - ROPE_DIM // 2, 1),
                            pltpu.roll(xn, ROPE_DIM // 2, 1))
        out = xn * cosf + partner * sinf
        if c < 4:
            qd_ref[0, :, c * LANES:(c + 1) * LANES] = out.astype(BF16)
        elif c < 8:
            kdt_ref[0, (c - 4) * LANES:(c - 3) * LANES, :] = out.T.astype(BF16)
        elif c < 12:
            qw_ref[0, :, (c - 8) * LANES:(c - 7) * LANES] = out.astype(BF16)
        else:
            kwt_ref[0] = out.T.astype(BF16)
    vw_ref[0] = pair[:, LANES:2 * LANES].astype(BF16)
    vd_ref[0] = jnp.dot(a, w_ref[0, :, OFF_VD:OFF_SU], preferred_element_type=F32).astype(BF16)
    su_ref[0] = jnp.dot(a, w_ref[0, :, OFF_SU:OFF_GT], preferred_element_type=F32)
    gt_ref[0] = _sigmoid(jnp.dot(a, w_ref[0, :, OFF_GT:], preferred_element_type=F32))


def _inproj(l, h, mods, n1g, w_in_p, gtab, cosf, sinf):
    b, s, d = h.shape
    d_in = w_in_p.shape[-1]
    nt = s // ROW_TILE
    mod_idx = lambda bi, j: (l, jnp.where(j == 0, b, bi), 0, 0)
    tile = lambda width: pl.BlockSpec((1, ROW_TILE, width), lambda bi, j: (bi, j, 0))
    out_shapes = (
        jax.ShapeDtypeStruct((b, s, 512), BF16),
        jax.ShapeDtypeStruct((b, 512, s), BF16),
        jax.ShapeDtypeStruct((b, s, WIN_WIDTH), BF16),
        jax.ShapeDtypeStruct((b, LANES, s), BF16),
        jax.ShapeDtypeStruct((b, s, DIFF_WIDTH), BF16),
        jax.ShapeDtypeStruct((b, s, LANES), BF16),
        jax.ShapeDtypeStruct((b, s, S5_WIDTH), F32),
        jax.ShapeDtypeStruct((b, s, 3 * d), F32),
    )
    tile_t = lambda width: pl.BlockSpec((1, width, ROW_TILE), lambda bi, j: (bi, 0, j))
    out_specs = (
        tile(512), tile_t(512), tile(WIN_WIDTH), tile_t(LANES),
        tile(DIFF_WIDTH), tile(LANES), tile(S5_WIDTH), tile(3 * d),
    )
    return pl.pallas_call(
        _inproj_kernel,
        grid=(b, nt),
        in_specs=[
            tile(d),
            pl.BlockSpec((1, 1, 6, d), mod_idx),
            _const_spec((1, 1, d), lambda bi, j: (l, 0, 0)),
            _const_spec((1, d, d_in), lambda bi, j: (l, 0, 0)),
            _const_spec((1, 1, N_NORM_CHUNKS * LANES), lambda bi, j: (l, 0, 0)),
            pl.BlockSpec((ROW_TILE, LANES), lambda bi, j: (j, 0)),
            pl.BlockSpec((ROW_TILE, LANES), lambda bi, j: (j, 0)),
        ],
        out_specs=out_specs,
        out_shape=out_shapes,
        compiler_params=_params("parallel", "parallel"),
        name="inproj",
    )(h, mods, n1g, w_in_p, gtab, cosf, sinf)


def _diff_kernel(lam_ref, og_ref, q_ref, kt_ref, v_ref, o_ref, *, lam_init, n_ctx):
    lp = lam_ref[0]
    lam = (jnp.exp(jnp.sum(lp[0:1] * lp[1:2], axis=-1, keepdims=True))
           - jnp.exp(jnp.sum(lp[2:3] * lp[3:4], axis=-1, keepdims=True)) + lam_init)
    og = og_ref[0]
    heads = q_ref.shape[2] // LANES

    def attend(n_keys):
        for hd in range(heads):
            v = v_ref[0, 0:n_keys, hd * DIFF_V_DIM:(hd + 1) * DIFF_V_DIM]
            v1 = jnp.concatenate([v, jnp.ones_like(v)], axis=1)
            outs = []
            for m in range(2):
                c0 = hd * LANES + m * DIFF_QK_DIM
                qm = q_ref[0, :, c0:c0 + DIFF_QK_DIM]
                km = kt_ref[0, c0:c0 + DIFF_QK_DIM, 0:n_keys]
                s = jnp.dot(qm, km, preferred_element_type=F32)
                e = jnp.exp2(s - jnp.max(s, axis=-1, keepdims=True))
                pv = jnp.dot(e.astype(BF16), v1, preferred_element_type=F32)
                outs.append(pv[:, 0:DIFF_V_DIM] / pv[:, DIFF_V_DIM:2 * DIFF_V_DIM])
            o = outs[0] - lam * outs[1]
            o_ref[0, :, hd * DIFF_V_DIM:(hd + 1) * DIFF_V_DIM] = (
                _rms(o, og) * (1.0 - lam_init)).astype(o_ref.dtype)

    is_ctx = pl.program_id(2) == 0

    @pl.when(is_ctx)
    def _():
        attend(n_ctx)

    @pl.when(jnp.logical_not(is_ctx))
    def _():
        attend(kt_ref.shape[2])


def _diff_attention(l, lam_init, n_ctx, lam_p, out_g, qd, kdt, vd):
    b, s, _ = qd.shape
    nt = s // ROW_TILE
    hw = DIFF_HEADS_PER_STEP * LANES
    kern = functools.partial(_diff_kernel, lam_init=lam_init, n_ctx=n_ctx)
    return pl.pallas_call(
        kern,
        grid=(b, DIFF_HEADS // DIFF_HEADS_PER_STEP, nt),
        in_specs=[
            _const_spec((1, 4, DIFF_QK_DIM), lambda bi, hd, j: (l, 0, 0)),
            _const_spec((1, 1, DIFF_V_DIM), lambda bi, hd, j: (l, 0, 0)),
            pl.BlockSpec((1, ROW_TILE, hw), lambda bi, hd, j: (bi, j, hd)),
            pl.BlockSpec((1, hw, s), lambda bi, hd, j: (bi, hd, 0)),
            pl.BlockSpec((1, s, hw), lambda bi, hd, j: (bi, 0, hd)),
        ],
        out_specs=pl.BlockSpec((1, ROW_TILE, hw), lambda bi, hd, j: (bi, j, hd)),
        out_shape=jax.ShapeDtypeStruct((b, s, DIFF_WIDTH), BF16),
        compiler_params=_params("parallel", "parallel", "parallel"),
        name="diff_attention",
    )(lam_p, out_g, qd, kdt, vd)


def _win_kernel(sink_ref, q_ref, kt_ref, v_ref, o_ref, *, layer, n_ctx):
    s_total = kt_ref.shape[2]
    band = 3 * QBLOCK
    rows = WIN_GROUP * QBLOCK
    row_group = lax.broadcasted_iota(jnp.int32, (rows, 1), 0) // QBLOCK
    blocks_per_step = q_ref.shape[1] // QBLOCK

    def run(with_band, sub):
        j = pl.program_id(1) * blocks_per_step + sub
        q = q_ref[0, sub * QBLOCK:(sub + 1) * QBLOCK, :]
        if with_band:
            start = pl.multiple_of(QBLOCK * jnp.minimum(j - 1, (s_total - band) // QBLOCK), QBLOCK)
            qpos = QBLOCK * j + lax.broadcasted_iota(jnp.int32, (QBLOCK, band), 0)
            kpos = start + lax.broadcasted_iota(jnp.int32, (QBLOCK, band), 1)
            valid = jnp.where(kpos >= n_ctx, jnp.abs(kpos - qpos), WINDOW + 1) <= WINDOW
            bias = jnp.concatenate([jnp.where(valid, 0.0, NEG_INF)] * WIN_GROUP, axis=0)
            v_b = v_ref[0, pl.ds(start, band), :]
            vb1 = jnp.concatenate([v_b, jnp.ones_like(v_b)], axis=1)
        v_c = v_ref[0, 0:n_ctx, :]
        vc1 = jnp.concatenate([v_c, jnp.ones_like(v_c)], axis=1)
        for kv in range(WIN_KV_HEADS):
            q4 = jnp.concatenate(
                [q[:, (kv * WIN_GROUP + g) * WIN_HEAD_DIM:(kv * WIN_GROUP + g + 1) * WIN_HEAD_DIM]
                 for g in range(WIN_GROUP)], axis=0)
            sink = jnp.zeros((rows, 1), F32)
            for g in range(WIN_GROUP):
                sink = jnp.where(row_group == g, sink_ref[layer, kv * WIN_GROUP + g] * LOG2E, sink)
            ksl = slice(kv * WIN_HEAD_DIM, (kv + 1) * WIN_HEAD_DIM)
            s_c = jnp.dot(q4, kt_ref[0, ksl, 0:n_ctx], preferred_element_type=F32)
            mx = jnp.maximum(jnp.max(s_c, axis=-1, keepdims=True), sink)
            if with_band:
                s_b = jnp.dot(q4, kt_ref[0, ksl, pl.ds(start, band)], preferred_element_type=F32) + bias
                mx = jnp.maximum(mx, jnp.max(s_b, axis=-1, keepdims=True))
            pv = jnp.dot(jnp.exp2(s_c - mx).astype(BF16), vc1, preferred_element_type=F32)
            if with_band:
                pv = pv + jnp.dot(jnp.exp2(s_b - mx).astype(BF16), vb1, preferred_element_type=F32)
            den = pv[:, LANES:LANES + WIN_HEAD_DIM] + jnp.exp2(sink - mx)
            o = pv[:, ksl] / den
            for g in range(WIN_GROUP):
                c0 = (kv * WIN_GROUP + g) * WIN_HEAD_DIM
                o_ref[0, sub * QBLOCK:(sub + 1) * QBLOCK, c0:c0 + WIN_HEAD_DIM] = (
                    o[g * QBLOCK:(g + 1) * QBLOCK].astype(o_ref.dtype))

    is_ctx = pl.program_id(1) < n_ctx // (QBLOCK * blocks_per_step)

    @pl.when(is_ctx)
    def _():
        for sub in range(blocks_per_step):
            run(False, sub)

    @pl.when(jnp.logical_not(is_ctx))
    def _():
        for sub in range(blocks_per_step):
            run(True, sub)


def _window_attention(l, n_ctx, sink, qw, kwt, vw):
    b, s, _ = qw.shape
    kern = functools.partial(_win_kernel, layer=l, n_ctx=n_ctx)
    return pl.pallas_call(
        kern,
        grid=(b, s // ROW_TILE),
        in_specs=[
            pl.BlockSpec(memory_space=pltpu.SMEM),
            pl.BlockSpec((1, ROW_TILE, WIN_WIDTH), lambda bi, j: (bi, j, 0)),
            pl.BlockSpec((1, LANES, s), lambda bi, j: (bi, 0, 0)),
            pl.BlockSpec((1, s, LANES), lambda bi, j: (bi, 0, 0)),
        ],
        out_specs=pl.BlockSpec((1, ROW_TILE, WIN_WIDTH), lambda bi, j: (bi, j, 0)),
        out_shape=jax.ShapeDtypeStruct((b, s, WIN_WIDTH), BF16),
        compiler_params=_params("parallel", "parallel"),
        name="window_attention",
    )(sink, qw, kwt, vw)


def _s5_kernel(uf_ref, ub_ref, bb_ref, cc_ref, are_ref, aim_ref, yf_ref, yb_ref,
               st_ref, slab_ref, buf_ref, of_ref, ob_ref):
    @pl.when(pl.program_id(0) == 0)
    def _():
        st_ref[...] = jnp.zeros_like(st_ref)

    nb, steps, _ = uf_ref.shape
    rows = steps * SUBLANES
    ns = S5_BLOCK_STATES
    fwd_sub = lax.broadcasted_iota(jnp.int32, (SUBLANES, LANES), 0) < nb
    fwd = (lax.broadcasted_iota(jnp.int32, (rows, 1), 0) % SUBLANES) < nb
    for jb in range(S5_BLOCKS):
        lanes = slice(jb * LANES, (jb + 1) * LANES)
        for k in range(nb):
            slab_ref[jb, k * S5_PITCH:k * S5_PITCH + steps, :] = uf_ref[k, :, lanes]
            slab_ref[jb, (nb + k) * S5_PITCH:(nb + k) * S5_PITCH + steps, :] = ub_ref[k, :, lanes]
        lhs_f, lhs_b = [], []
        for i in range(steps):
            a = slab_ref[jb, pl.ds(i, SUBLANES, stride=S5_PITCH), :]
            b = slab_ref[jb, pl.ds(steps - 1 - i, SUBLANES, stride=S5_PITCH), :]
            lhs_f.append(jnp.where(fwd_sub, a, 0.0))
            lhs_b.append(jnp.where(fwd_sub, 0.0, b))
        buf = buf_ref.at[jb % 2]
        piece = steps // S5_PIECES
        for p in range(S5_PIECES):
            sl = slice(p * piece, (p + 1) * piece)
            lhs = jnp.concatenate([jnp.concatenate(lhs_f[sl], axis=0).astype(BF16),
                                   jnp.concatenate(lhs_b[sl], axis=0).astype(BF16)], axis=1)
            buf[p * piece * SUBLANES:(p + 1) * piece * SUBLANES, :] = jnp.dot(
                lhs, bb_ref[0, jb], preferred_element_type=F32)
        a_re = are_ref[0, :, jb * ns:(jb + 1) * ns]
        a_im = aim_ref[0, :, jb * ns:(jb + 1) * ns]
        s_re = st_ref[0, :, jb * ns:(jb + 1) * ns]
        s_im = st_ref[1, :, jb * ns:(jb + 1) * ns]
        for t in range(steps):
            r = slice(t * SUBLANES, (t + 1) * SUBLANES)
            n_re = a_re * s_re - a_im * s_im + buf[r, 0:ns]
            n_im = a_re * s_im + a_im * s_re + buf[r, ns:2 * ns]
            buf[r, 0:ns] = n_re
            buf[r, ns:2 * ns] = n_im
            s_re, s_im = n_re, n_im
        st_ref[0, :, jb * ns:(jb + 1) * ns] = s_re
        st_ref[1, :, jb * ns:(jb + 1) * ns] = s_im
        for p in range(S5_PIECES):
            r0 = p * piece * SUBLANES
            y = jnp.dot(buf[r0:r0 + piece * SUBLANES, :].astype(BF16), cc_ref[0, jb],
                        preferred_element_type=F32)
            y = jnp.where(fwd[0:piece * SUBLANES], y[:, 0:LANES], y[:, LANES:2 * LANES])
            for i in range(piece):
                v = y[i * SUBLANES:(i + 1) * SUBLANES]
                t = p * piece + i
                of_ref[jb, pl.ds(t, SUBLANES, stride=S5_PITCH), :] = v
                ob_ref[jb, pl.ds(steps - 1 - t, SUBLANES, stride=S5_PITCH), :] = v
        for k in range(nb):
            yf_ref[k, :, lanes] = of_ref[jb, k * S5_PITCH:k * S5_PITCH + steps, :]
            yb_ref[k, :, lanes] = ob_ref[jb, (nb + k) * S5_PITCH:(nb + k) * S5_PITCH + steps, :]


def _s5_scan(l, n_ctx, su, bbcat, ccat, a_re, a_im):
    b, s, width = su.shape
    rows = S5_CHUNK * SUBLANES
    n_states = S5_GROUPS * S5_STATE
    n_tiles = s // S5_CHUNK
    ctx_tiles = n_ctx // S5_CHUNK
    mirror = lambda c: jnp.where(c < ctx_tiles, ctx_tiles - 1 - c, n_tiles + ctx_tiles - 1 - c)
    fwd_spec = pl.BlockSpec((b, S5_CHUNK, width), lambda c: (0, c, 0))
    bwd_spec = pl.BlockSpec((b, S5_CHUNK, width), lambda c: (0, mirror(c), 0))
    slab = pltpu.VMEM((width // LANES, SUBLANES * S5_PITCH, LANES), F32)
    return pl.pallas_call(
        _s5_kernel,
        grid=(n_tiles,),
        in_specs=[
            fwd_spec, bwd_spec,
            _const_spec((1, S5_BLOCKS, 2 * LANES, 2 * S5_BLOCK_STATES), lambda c: (l, 0, 0, 0)),
            _const_spec((1, S5_BLOCKS, 2 * S5_BLOCK_STATES, 2 * LANES), lambda c: (l, 0, 0, 0)),
            _const_spec((1, SUBLANES, n_states), lambda c: (l, 0, 0)),
            _const_spec((1, SUBLANES, n_states), lambda c: (l, 0, 0)),
        ],
        out_specs=(fwd_spec, bwd_spec),
        out_shape=(jax.ShapeDtypeStruct((b, s, width), F32), jax.ShapeDtypeStruct((b, s, width), F32)),
        scratch_shapes=[
            pltpu.VMEM((2, SUBLANES, n_states), F32),
            slab,
            pltpu.VMEM((2, rows, 2 * S5_BLOCK_STATES), F32),
            slab, slab,
        ],
        compiler_params=_params("arbitrary"),
        name="s5_scan",
    )(su, su, bbcat, ccat, a_re, a_im)


def _gelu_tanh(x):
    return 0.5 * x * (1.0 + jnp.tanh(math.sqrt(2.0 / math.pi) * (x + 0.044715 * (x * x * x))))


def _merge_ffn_kernel(h_ref, mod_ref, yd_ref, su_ref, yf_ref, yb_ref, yw_ref, gt_ref,
                      dsk_ref, wglu_ref, wd_ref, ws_ref, ww_ref, wo_ref, n2g_ref, w1_ref, w2_ref,
                      o_ref, *, latent_only):
    def run():
        d = h_ref.shape[2]
        mod = mod_ref[0, 0]
        y = su_ref[0] * dsk_ref[0] + yf_ref[0] + yb_ref[0]
        g = _gelu_tanh(y)
        ys = g * _sigmoid(jnp.dot(g.astype(BF16), wglu_ref[0], preferred_element_type=F32))
        m = (gt_ref[0, :, 0:d] * jnp.dot(yd_ref[0], wd_ref[0], preferred_element_type=F32)
             + gt_ref[0, :, d:2 * d] * jnp.dot(ys.astype(BF16), ws_ref[0], preferred_element_type=F32)
             + gt_ref[0, :, 2 * d:3 * d] * jnp.dot(yw_ref[0], ww_ref[0], preferred_element_type=F32))
        x = h_ref[0] + mod[2:3] * jnp.dot(m.astype(BF16), wo_ref[0], preferred_element_type=F32)
        f = (_rms(x, n2g_ref[0]) * (1.0 + mod[4:5]) + mod[3:4]).astype(BF16)
        t = jnp.maximum(jnp.dot(f, w1_ref[0], preferred_element_type=F32), 0.0)
        t = (t * t).astype(BF16)
        o_ref[0] = x + mod[5:6] * jnp.dot(t, w2_ref[0], preferred_element_type=F32)

    if latent_only:
        pl.when(pl.program_id(1) > 0)(run)
    else:
        run()


def _merge_ffn(l, h, mods, yd, su, yf, yb, yw, gates, dsk, wglu, wd, ws, ww, wo, n2g, w1, w2, latent_only):
    b, s, d = h.shape
    nt = s // ROW_TILE
    d_ff = w1.shape[-1]
    mod_idx = lambda bi, j: (l, jnp.where(j == 0, b, bi), 0, 0)
    tile = lambda width: pl.BlockSpec((1, ROW_TILE, width), lambda bi, j: (bi, j, 0))
    wspec = lambda r, c: _const_spec((1, r, c), lambda bi, j: (l, 0, 0))
    if latent_only:
        out_spec = pl.BlockSpec((1, ROW_TILE, d), lambda bi, j: (bi, jnp.maximum(j - 1, 0), 0))
        out_rows = s - ROW_TILE
    else:
        out_spec, out_rows = tile(d), s
    return pl.pallas_call(
        functools.partial(_merge_ffn_kernel, latent_only=latent_only),
        grid=(b, nt),
        in_specs=[
            tile(d), pl.BlockSpec((1, 1, 6, d), mod_idx),
            tile(DIFF_WIDTH), tile(S5_WIDTH), tile(S5_WIDTH), tile(S5_WIDTH), tile(WIN_WIDTH), tile(3 * d),
            wspec(1, S5_WIDTH), wspec(S5_WIDTH, S5_WIDTH),
            wspec(DIFF_WIDTH, d), wspec(S5_WIDTH, d), wspec(WIN_WIDTH, d), wspec(d, d),
            wspec(1, d), wspec(d, d_ff), wspec(d_ff, d),
        ],
        out_specs=out_spec,
        out_shape=jax.ShapeDtypeStruct((b, out_rows, d), F32),
        compiler_params=pltpu.CompilerParams(dimension_semantics=("parallel", "arbitrary"),
                                             vmem_limit_bytes=MERGE_FFN_VMEM_LIMIT),
        name="merge_ffn",
    )(h, mods, yd, su, yf, yb, yw, gates, dsk, wglu, wd, ws, ww, wo, n2g, w1, w2)


def _permute_inproj(w_in):
    depth, d, _ = w_in.shape
    qk = w_in[:, :, 0:1024].reshape(depth, d, 2, 2, DIFF_HEADS, DIFF_QK_DIM)
    qk = qk.transpose(0, 1, 2, 4, 3, 5).reshape(depth, d, 1024)
    parts = [qk, w_in[:, :, 2048:2816], w_in[:, :, 1024:1536], w_in[:, :, 1536:2048], w_in[:, :, 2816:]]
    return jnp.concatenate(parts, -1).astype(BF16)


def _rope_tables(n_ctx, n_lat):
    n_freq = ROPE_DIM // 4
    inv = ROPE_BASE ** (-jnp.arange(n_freq, dtype=F32) / n_freq)
    rows = n_lat // GRID_W
    r = jnp.repeat(jnp.arange(rows, dtype=F32), GRID_W)
    col = jnp.tile(jnp.arange(GRID_W, dtype=F32), rows)
    ang = jnp.concatenate([r[:, None] * inv, col[:, None] * inv], -1)
    cos, sin = jnp.cos(ang), jnp.sin(ang)
    cosf = jnp.concatenate([jnp.ones((n_ctx, LANES), F32), jnp.tile(cos, (1, 4))], 0)
    sinf = jnp.concatenate([jnp.zeros((n_ctx, LANES), F32),
                            jnp.tile(jnp.concatenate([-sin, sin], -1), (1, 2))], 0)
    return cosf, sinf


def _s5_disc_kernel(lre_ref, lim_ref, ldt_ref, bre_ref, bim_ref, are_ref, aim_ref, bbre_ref, bbim_ref):
    lre = lre_ref[...]
    lim = lim_ref[...]
    dt = jnp.exp(ldt_ref[...])
    mag = jnp.exp(lre * dt)
    ang = lim * dt
    a_re = mag * jnp.cos(ang)
    a_im = mag * jnp.sin(ang)
    den = lre * lre + lim * lim
    n_re = a_re - 1.0
    f_re = (n_re * lre + a_im * lim) / den
    f_im = (a_im * lre - n_re * lim) / den
    are_ref[...] = a_re
    aim_ref[...] = a_im
    b_re = bre_ref[...]
    b_im = bim_ref[...]
    bbre_ref[...] = f_re * b_re - f_im * b_im
    bbim_ref[...] = f_re * b_im + f_im * b_re


def _s5_discretize(lam_re, lam_im, log_dt, b_re, b_im):
    depth, two, g, p = lam_re.shape
    hg = b_re.shape[-1]
    r = depth * two * g
    row = lambda a: a.reshape(r, 1, p)
    bt = lambda a: a.reshape(r, p, hg).transpose(0, 2, 1)
    vec = jax.ShapeDtypeStruct((r, 1, p), F32)
    mat = jax.ShapeDtypeStruct((r, hg, p), F32)
    a_re, a_im, bb_re, bb_im = pl.pallas_call(
        _s5_disc_kernel, out_shape=(vec, vec, mat, mat), name="s5_discretize",
    )(row(lam_re), row(lam_im), log_dt.reshape(r, 1, 1), bt(b_re), bt(b_im))
    shape = (depth, two, g, hg, p)
    return (a_re.reshape(depth, two, g, p), a_im.reshape(depth, two, g, p),
            bb_re.reshape(shape), bb_im.reshape(shape))


def _s5_tables(lam_re, lam_im, log_dt, b_re, b_im, c_re, c_im):
    depth = lam_re.shape[0]
    a_re, a_im, bb_re, bb_im = _s5_discretize(lam_re, lam_im, log_dt, b_re, b_im)
    bb_re = bb_re.transpose(0, 1, 2, 4, 3)
    bb_im = bb_im.transpose(0, 1, 2, 4, 3)
    eye = jnp.eye(S5_BLOCK_GROUPS, dtype=F32)
    shp = (depth, 2, S5_BLOCKS, S5_BLOCK_GROUPS, S5_STATE, S5_GROUP)

    def in_block(bb):
        t = jnp.einsum('ldjgph,gk->ldjghkp', bb.reshape(shp), eye)
        return t.reshape(depth, 2, S5_BLOCKS, LANES, S5_BLOCK_STATES).transpose(0, 2, 1, 3, 4)

    bbcat = jnp.concatenate([in_block(bb_re), in_block(bb_im)], -1)
    bbcat = bbcat.reshape(depth, S5_BLOCKS, 2 * LANES, 2 * S5_BLOCK_STATES).astype(BF16)
    shc = (depth, 2, S5_BLOCKS, S5_BLOCK_GROUPS, S5_GROUP, S5_STATE)

    def out_block(c):
        t = jnp.einsum('ldjghp,gk->ljgpdkh', c.reshape(shc), eye)
        return t.reshape(depth, S5_BLOCKS, S5_BLOCK_STATES, 2 * LANES)

    ccat = jnp.concatenate([out_block(c_re), out_block(-c_im)], 2).astype(BF16)
    n_states = S5_GROUPS * S5_STATE
    rep = lambda a: jnp.repeat(a.reshape(depth, 2, n_states), SUBLANES // 2, axis=1)
    return bbcat, ccat, rep(a_re), rep(a_im)


def kernel(x, c, ctx, c_ctx, w_mod, b_mod, norm1_g, norm2_g, w_in, diff_q_norm_g, diff_k_norm_g, diff_lam_q1, diff_lam_k1, diff_lam_q2, diff_lam_k2, diff_out_norm_g, s5_lambda_re, s5_lambda_im, s5_log_dt, s5_b_re, s5_b_im, s5_c_re, s5_c_im, s5_d, s5_w_glu, win_q_norm_g, win_k_norm_g, win_sink, w_proj_diff, w_proj_s5, w_proj_win, w_out, w_ff1, w_ff2):
    b, n_lat, d = x.shape
    n_ctx = ctx.shape[1]
    s = n_ctx + n_lat
    depth = w_in.shape[0]
    assert n_ctx == ROW_TILE and n_lat % ROW_TILE == 0 and b == SUBLANES // 2

    cc = jnp.zeros((SUBLANES, d), F32).at[:b].set(c).at[b].set(c_ctx)
    mods = _modulation(cc, w_mod, b_mod).reshape(depth, SUBLANES, 6, d)

    w_in_p = _permute_inproj(w_in)
    scale_d = DIFF_QK_DIM ** -0.5 * LOG2E
    scale_w = WIN_HEAD_DIM ** -0.5 * LOG2E
    gtab = jnp.concatenate([
        jnp.tile(diff_q_norm_g * scale_d, (1, 8)), jnp.tile(diff_k_norm_g, (1, 8)),
        jnp.tile(win_q_norm_g * scale_w, (1, 8)), jnp.tile(win_k_norm_g, (1, 2))], -1)[:, None, :]
    cosf, sinf = _rope_tables(n_ctx, n_lat)
    lam_p = jnp.stack([diff_lam_q1, diff_lam_k1, diff_lam_q2, diff_lam_k2], 1)
    bbcat, ccat, a_re, a_im = _s5_tables(s5_lambda_re, s5_lambda_im, s5_log_dt, s5_b_re, s5_b_im,
                                         s5_c_re, s5_c_im)
    bf = lambda w: w.astype(BF16)
    wglu, wd, ws, ww, wo, w1, w2 = (bf(s5_w_glu), bf(w_proj_diff), bf(w_proj_s5), bf(w_proj_win),
                                    bf(w_out), bf(w_ff1), bf(w_ff2))
    n1g, n2g = norm1_g[:, None, :], norm2_g[:, None, :]
    og, dsk = diff_out_norm_g[:, None, :], s5_d[:, None, :]

    h = jnp.concatenate([ctx, x], axis=1)
    for l in range(depth):
        lam_init = 0.8 - 0.6 * math.exp(-0.3 * l)
        qd, kdt, qw, kwt, vd, vw, su, gates = _inproj(l, h, mods, n1g, w_in_p, gtab, cosf, sinf)
        yd = _diff_attention(l, lam_init, n_ctx, lam_p, og, qd, kdt, vd)
        yw = _window_attention(l, n_ctx, win_sink, qw, kwt, vw)
        yf, yb = _s5_scan(l, n_ctx, su, bbcat, ccat, a_re, a_im)
        h = _merge_ffn(l, h, mods, yd, su, yf, yb, yw, gates, dsk, wglu, wd, ws, ww, wo, n2g, w1, w2,
                       latent_only=(l == depth - 1))
    return h
```
